```python
import math
import jax, jax.numpy as jnp
from jax import lax
import numpy as np

D_MODEL = 1024
BATCH = 2
SEQ = 8192
DEPTH = 1

MEM_LEN = 256
EPS = 1e-6
SB_HEADS = 8
SB_HEAD_DIM = 64
SB_WIDTH = SB_HEADS * SB_HEAD_DIM
CONV_CH = D_MODEL - SB_WIDTH
CONV_GROUPS = 8
CONV_K = 3
MIX_WIDTH = SB_WIDTH + CONV_CH
IN_COLS = 3 * SB_WIDTH + 3 * CONV_CH
Q_BLOCK = 128
MEM_HEADS = 4
MEM_HEAD_DIM = D_MODEL // MEM_HEADS
PEER_HEADS = 8
PEER_KEYS = 128
PEER_EXPERTS = PEER_KEYS * PEER_KEYS
PEER_QDIM = 256
PEER_HALF = PEER_QDIM // 2
PEER_TOPK = 16
PEER_TOK_BLOCK = 128

kernel_name = "hybrid_sb_attn_shortconv_peer"


def rmsnorm(x, g):
    xf = x.astype(jnp.float32)
    xf = xf * lax.rsqrt(jnp.mean(xf * xf, axis=-1, keepdims=True) + EPS)
    return xf.astype(x.dtype) * g


def stick_breaking_attention(q, k, v):
    b, h, s, dh = q.shape
    nblk = s // Q_BLOCK
    qb = q.reshape(b, h, nblk, Q_BLOCK, dh).transpose(2, 0, 1, 3, 4)
    key_pos = jnp.arange(s)
    scale = 1.0 / math.sqrt(dh)

    def block(args):
        qi, i = args
        z = jnp.einsum('bhqd,bhkd->bhqk', qi, k).astype(jnp.float32) * scale
        q_pos = i * Q_BLOCK + jnp.arange(Q_BLOCK)
        past = key_pos[None, :] < q_pos[:, None]
        log_beta = jax.nn.log_sigmoid(z)
        log_1m = jnp.where(past, jax.nn.log_sigmoid(-z), 0.0)
        tail = lax.cumsum(log_1m, axis=3, reverse=True) - log_1m
        a = jnp.where(past, jnp.exp(log_beta + tail), 0.0)
        return jnp.einsum('bhqk,bhkd->bhqd', a.astype(v.dtype), v)

    out = lax.map(block, (qb, jnp.arange(nblk)))
    return out.transpose(1, 0, 3, 2, 4).reshape(b, s, h * dh)


def causal_short_conv(xc, w):
    c = xc.shape[-1]
    return lax.conv_general_dilated(
        xc, w[:, None, :].astype(xc.dtype), window_strides=(1,),
        padding=[(CONV_K - 1, 0)], dimension_numbers=('NWC', 'WIO', 'NWC'),
        feature_group_count=c)


def token_mixer(n, w_in, conv_w, g_sb_out, g_conv_out, w_out):
    b, s, _ = n.shape
    proj = n @ w_in
    cuts = [SB_WIDTH, 2 * SB_WIDTH, 3 * SB_WIDTH,
            3 * SB_WIDTH + CONV_CH, 3 * SB_WIDTH + 2 * CONV_CH]
    q, k, v, gate_b, gate_c, xin = jnp.split(proj, cuts, axis=-1)
    to_heads = lambda t: t.reshape(b, s, SB_HEADS, SB_HEAD_DIM).transpose(0, 2, 1, 3)
    sb = stick_breaking_attention(to_heads(q), to_heads(k), to_heads(v))
    conv = gate_b * causal_short_conv(gate_c * xin, conv_w)
    mixed = jnp.concatenate([rmsnorm(sb, g_sb_out), rmsnorm(conv, g_conv_out)], axis=-1)
    return mixed @ w_out


def memory_cross_attention(n, mem_n, w_q_mem, w_kv_mem, w_o_mem):
    b, s, _ = n.shape
    m = mem_n.shape[1]
    q = (n @ w_q_mem).reshape(b, s, MEM_HEADS, MEM_HEAD_DIM)
    k, v = jnp.split(mem_n @ w_kv_mem, 2, axis=-1)
    k = k.reshape(b, m, MEM_HEADS, MEM_HEAD_DIM)
    v = v.reshape(b, m, MEM_HEADS, MEM_HEAD_DIM)
    sc = jnp.einsum('bshd,bmhd->bhsm', q, k).astype(jnp.float32) / math.sqrt(MEM_HEAD_DIM)
    p = jax.nn.softmax(sc, axis=-1).astype(v.dtype)
    o = jnp.einsum('bhsm,bmhd->bshd', p, v).reshape(b, s, MEM_HEADS * MEM_HEAD_DIM)
    return o @ w_o_mem


def peer_layer(n, w_query, sub_keys, expert_u, expert_v):
    b, s, d = n.shape
    xt = n.reshape((b * s) // PEER_TOK_BLOCK, PEER_TOK_BLOCK, d)

    def block(xb):
        tb = xb.shape[0]
        q = (xb @ w_query).reshape(tb, PEER_HEADS, 2, PEER_HALF)
        half_s = jnp.einsum('thcd,hcnd->thcn', q, sub_keys).astype(jnp.float32)
        top_s, top_i = lax.top_k(half_s, PEER_TOPK)
        cand_s = top_s[:, :, 0, :, None] + top_s[:, :, 1, None, :]
        cand_i = top_i[:, :, 0, :, None] * PEER_KEYS + top_i[:, :, 1, None, :]
        cand_s = cand_s.reshape(tb, PEER_HEADS, PEER_TOPK * PEER_TOPK)
        cand_i = cand_i.reshape(tb, PEER_HEADS, PEER_TOPK * PEER_TOPK)
        best_s, best_j = lax.top_k(cand_s, PEER_TOPK)
        expert = jnp.take_along_axis(cand_i, best_j, axis=-1)
        gate = jax.nn.softmax(best_s, axis=-1)
        u = expert_u[expert]
        act = jax.nn.gelu(jnp.einsum('thkd,td->thk', u, xb).astype(jnp.float32), approximate=False)
        coef = (gate * act).astype(xb.dtype)
        vv = expert_v[expert]
        return jnp.einsum('thk,thkd->td', coef, vv)

    return lax.map(block, xt).reshape(b, s, d)


def setup_inputs(seed: int = 0) -> dict:
    key = jax.random.key(seed)
    ks = jax.random.split(key, 20)
    f32 = jnp.float32
    nrm = lambda k, shape, sc: jax.random.normal(k, shape, f32) * sc
    gain = lambda k, shape: 1.0 + 0.02 * jax.random.normal(k, shape, f32)
    L = DEPTH
    return {
        "x": jax.random.normal(ks[0], (BATCH, SEQ, D_MODEL), f32),
        "mem": jax.random.normal(ks[1], (BATCH, MEM_LEN, D_MODEL), f32),
        "g_mix": gain(ks[2], (L, D_MODEL)),
        "w_in": nrm(ks[3], (L, D_MODEL, IN_COLS), D_MODEL ** -0.5),
        "conv_w": nrm(ks[4], (L, CONV_K, CONV_CH), CONV_K ** -0.5),
        "g_sb_out": gain(ks[5], (L, SB_WIDTH)),
        "g_conv_out": gain(ks[6], (L, CONV_CH)),
        "w_out": nrm(ks[7], (L, MIX_WIDTH, D_MODEL), MIX_WIDTH ** -0.5),
        "g_xattn": gain(ks[8], (L, D_MODEL)),
        "g_mem": gain(ks[9], (L, D_MODEL)),
        "w_q_mem": nrm(ks[10], (L, D_MODEL, MEM_HEADS * MEM_HEAD_DIM), D_MODEL ** -0.5),
        "w_kv_mem": nrm(ks[11], (L, D_MODEL, 2 * MEM_HEADS * MEM_HEAD_DIM), D_MODEL ** -0.5),
        "w_o_mem": nrm(ks[12], (L, MEM_HEADS * MEM_HEAD_DIM, D_MODEL), D_MODEL ** -0.5),
        "g_ffn": gain(ks[13], (L, D_MODEL)),
        "w_query": nrm(ks[14], (L, D_MODEL, PEER_HEADS * PEER_QDIM), D_MODEL ** -0.5),
        "sub_keys": nrm(ks[15], (L, PEER_HEADS, 2, PEER_KEYS, PEER_HALF), PEER_HALF ** -0.5),
        "expert_u": nrm(ks[16], (L, PEER_EXPERTS, D_MODEL), D_MODEL ** -0.5),
        "expert_v": nrm(ks[17], (L, PEER_EXPERTS, D_MODEL), (PEER_HEADS * PEER_TOPK) ** -0.5),
        "g_final": gain(ks[18], (D_MODEL,)),
    }


def reference(x, mem, g_mix, w_in, conv_w, g_sb_out, g_conv_out, w_out,
              g_xattn, g_mem, w_q_mem, w_kv_mem, w_o_mem,
              g_ffn, w_query, sub_keys, expert_u, expert_v, g_final):
    h = x
    for l in range(DEPTH):
        h = h + token_mixer(rmsnorm(h, g_mix[l]), w_in[l], conv_w[l],
                            g_sb_out[l], g_conv_out[l], w_out[l])
        h = h + memory_cross_attention(rmsnorm(h, g_xattn[l]), rmsnorm(mem, g_mem[l]),
                                       w_q_mem[l], w_kv_mem[l], w_o_mem[l])
        h = h + peer_layer(rmsnorm(h, g_ffn[l]), w_query[l], sub_keys[l],
                           expert_u[l], expert_v[l])
    return rmsnorm(h, g_final)
```

```python
import functools
import math

import jax
import jax.numpy as jnp
from jax import lax
from jax.experimental import pallas as pl
from jax.experimental.pallas import tpu as pltpu

F32 = jnp.float32
BF16 = jnp.bfloat16

D_MODEL = 1024
EPS = 1e-6
SB_HEADS = 8
SB_HEAD_DIM = 64
SB_WIDTH = SB_HEADS * SB_HEAD_DIM
CONV_CH = D_MODEL - SB_WIDTH
CONV_K = 3
IN_COLS = 3 * SB_WIDTH + 3 * CONV_CH
MEM_HEADS = 4
MEM_HEAD_DIM = D_MODEL // MEM_HEADS
PEER_HEADS = 8
PEER_KEYS = 128
PEER_EXPERTS = PEER_KEYS * PEER_KEYS
PEER_QDIM = 256
PEER_HALF = PEER_QDIM // 2
PEER_TOPK = 16

LANES = 128
SUBLANES = 8

TOK_TILE = 512
ATT_BLOCK = 256
HEADS_PER_STEP = LANES // SB_HEAD_DIM
PEER_TOK = 512
PEER_LC = PEER_TOK // LANES
PEER_ECHUNK = 1024
PEER_ROWS = PEER_ECHUNK // PEER_KEYS
VMEM_LIMIT = 58 * 1024 * 1024


def _rms(x, g):
    return x * lax.rsqrt(jnp.mean(x * x, axis=-1, keepdims=True) + EPS) * g


def _dot(a, b):
    return jnp.dot(a, b, preferred_element_type=F32)


def _dot_nt(a, b):
    return lax.dot_general(a, b, (((1,), (1,)), ((), ())), preferred_element_type=F32)


def _inproj_kernel(x_ref, g_ref, w_ref, qT_ref, k_ref, vT_ref, gb_ref, cx_ref):
    n = _rms(x_ref[0], g_ref[...]).astype(BF16)
    proj = _dot(n, w_ref[...])
    q = proj[:, 0:SB_WIDTH] * (SB_HEAD_DIM ** -0.5)
    qT_ref[0] = q.T.astype(BF16)
    k_ref[0] = proj[:, SB_WIDTH:2 * SB_WIDTH].astype(BF16)
    vT = proj[:, 2 * SB_WIDTH:3 * SB_WIDTH].T.astype(BF16)
    for j in range(TOK_TILE // ATT_BLOCK):
        vT_ref[0, j] = vT[:, j * ATT_BLOCK:(j + 1) * ATT_BLOCK]
    o = 3 * SB_WIDTH
    gb_ref[0] = proj[:, o:o + CONV_CH]
    cx_ref[0] = proj[:, o + CONV_CH:o + 2 * CONV_CH] * proj[:, o + 2 * CONV_CH:o + 3 * CONV_CH]


def _in_proj(x, g_mix, w_in):
    b, s, d = x.shape
    nt = s // TOK_TILE
    return pl.pallas_call(
        _inproj_kernel,
        grid=(b, nt),
        in_specs=[
            pl.BlockSpec((1, TOK_TILE, d), lambda bi, i: (bi, i, 0)),
            pl.BlockSpec((1, d), lambda bi, i: (0, 0)),
            pl.BlockSpec((d, IN_COLS), lambda bi, i: (0, 0)),
        ],
        out_specs=[
            pl.BlockSpec((1, SB_WIDTH, TOK_TILE), lambda bi, i: (bi, 0, i)),
            pl.BlockSpec((1, TOK_TILE, SB_WIDTH), lambda bi, i: (bi, i, 0)),
            pl.BlockSpec((1, TOK_TILE // ATT_BLOCK, SB_WIDTH, ATT_BLOCK), lambda bi, i: (bi, i, 0, 0)),
            pl.BlockSpec((1, TOK_TILE, CONV_CH), lambda bi, i: (bi, i, 0)),
            pl.BlockSpec((1, TOK_TILE, CONV_CH), lambda bi, i: (bi, i, 0)),
        ],
        out_shape=[
            jax.ShapeDtypeStruct((b, SB_WIDTH, s), BF16),
            jax.ShapeDtypeStruct((b, s, SB_WIDTH), BF16),
            jax.ShapeDtypeStruct((b, s // ATT_BLOCK, SB_WIDTH, ATT_BLOCK), BF16),
            jax.ShapeDtypeStruct((b, s, CONV_CH), F32),
            jax.ShapeDtypeStruct((b, s, CONV_CH), F32),
        ],
        compiler_params=pltpu.CompilerParams(vmem_limit_bytes=VMEM_LIMIT),
        name="in_proj",
    )(x, g_mix.reshape(1, d), w_in)


def _attn_kernel(qT_ref, k_ref, vT_ref, u_ref, o_ref):
    qi = pl.program_id(2)
    tb = ATT_BLOCK
    qT2 = qT_ref[0]
    feat = lax.broadcasted_iota(jnp.int32, (LANES, tb), 0)
    key_pos = lax.broadcasted_iota(jnp.int32, (tb, tb), 0)
    qry_pos = lax.broadcasted_iota(jnp.int32, (tb, tb), 1)
    past = key_pos < qry_pos

    for hh in range(HEADS_PER_STEP):
        lo_f = hh * SB_HEAD_DIM
        in_head = (feat >= lo_f) & (feat < lo_f + SB_HEAD_DIM)
        qTm = jnp.where(in_head, qT2, jnp.zeros_like(qT2))

        def block(jb, c, acc, diagonal):
            kb = k_ref[0, pl.ds(pl.multiple_of(jb * tb, tb), tb), :]
            z = _dot(kb, qTm)
            sp = jnp.log(1.0 + jnp.exp(-jnp.abs(z)))
            log_beta = jnp.minimum(z, 0.0) - sp
            log_1m = log_beta - z
            if diagonal:
                log_1m = jnp.where(past, log_1m, 0.0)
            hi = log_1m.astype(BF16)
            lo = (log_1m - hi.astype(F32)).astype(BF16)
            tail = _dot(u_ref[...], hi) + _dot(u_ref[...], lo)
            a = jnp.exp(log_beta + tail + c)
            if diagonal:
                a = jnp.where(past, a, 0.0)
            vb = vT_ref[0, jb, lo_f:lo_f + SB_HEAD_DIM, :]
            acc = acc + _dot(vb, a.astype(BF16))
            c = c + tail[0:1, :] + log_1m[0:1, :]
            return c, acc

        c = jnp.zeros((1, tb), F32)
        acc = jnp.zeros((SB_HEAD_DIM, tb), F32)
        c, acc = block(qi, c, acc, True)
        c, acc = lax.fori_loop(
            0, qi, lambda j, ca: block(qi - 1 - j, ca[0], ca[1], False), (c, acc))
        o_ref[0, lo_f:lo_f + SB_HEAD_DIM, :] = acc


def _attention(qT, k, vT):
    b, _, s = qT.shape
    nq = s // ATT_BLOCK
    hp = SB_WIDTH // LANES
    tri = (lax.broadcasted_iota(jnp.int32, (ATT_BLOCK, ATT_BLOCK), 1)
           > lax.broadcasted_iota(jnp.int32, (ATT_BLOCK, ATT_BLOCK), 0)).astype(BF16)
    return pl.pallas_call(
        _attn_kernel,
        grid=(b, hp, nq),
        in_specs=[
            pl.BlockSpec((1, LANES, ATT_BLOCK), lambda bi, h, q: (bi, h, q)),
            pl.BlockSpec((1, s, LANES), lambda bi, h, q: (bi, 0, h)),
            pl.BlockSpec((1, nq, LANES, ATT_BLOCK), lambda bi, h, q: (bi, 0, h, 0)),
            pl.BlockSpec((ATT_BLOCK, ATT_BLOCK), lambda bi, h, q: (0, 0)),
        ],
        out_specs=pl.BlockSpec((1, LANES, ATT_BLOCK), lambda bi, h, q: (bi, h, q)),
        out_shape=jax.ShapeDtypeStruct((b, SB_WIDTH, s), F32),
        compiler_params=pltpu.CompilerParams(vmem_limit_bytes=VMEM_LIMIT),
        name="sb_attention",
    )(qT, k, vT, tri)


def _memkv_kernel(mem_ref, g_ref, w_ref, k_ref, v_ref):
    mn = _rms(mem_ref[0], g_ref[...]).astype(BF16)
    kv = _dot(mn, w_ref[...])
    k_ref[0] = kv[:, :D_MODEL].astype(BF16)
    v_ref[0] = kv[:, D_MODEL:].astype(BF16)


def _mem_kv(mem, g_mem, w_kv):
    b, m, d = mem.shape
    return pl.pallas_call(
        _memkv_kernel,
        grid=(b,),
        in_specs=[
            pl.BlockSpec((1, m, d), lambda bi: (bi, 0, 0)),
            pl.BlockSpec((1, d), lambda bi: (0, 0)),
            pl.BlockSpec((d, 2 * d), lambda bi: (0, 0)),
        ],
        out_specs=[pl.BlockSpec((1, m, d), lambda bi: (bi, 0, 0))] * 2,
        out_shape=[jax.ShapeDtypeStruct((b, m, d), BF16)] * 2,
        compiler_params=pltpu.CompilerParams(vmem_limit_bytes=VMEM_LIMIT),
        name="mem_kv",
    )(mem, g_mem.reshape(1, d), w_kv)


def _post_kernel(sbT_ref, gb_ref, cx_ref, halo_ref, x_ref, km_ref, vm_ref, cw_ref,
                 gsb_ref, gcv_ref, wout_ref, gx_ref, wq_ref, wo_ref, gffn_ref,
                 h2_ref, n3_ref):
    i = pl.program_id(1)
    tm = TOK_TILE
    sbn = _rms(sbT_ref[0].T, gsb_ref[...])

    cx = cx_ref[0]
    halo = jnp.where(i > 0, halo_ref[0], 0.0)
    row = lax.broadcasted_iota(jnp.int32, (tm, CONV_CH), 0)
    cx1 = jnp.where(row == 0, halo[7:8], pltpu.roll(cx, 1, 0))
    cx2 = jnp.where(row == 0, halo[6:7],
                    jnp.where(row == 1, halo[7:8], pltpu.roll(cx, 2, 0)))
    cw = cw_ref[...]
    conv = gb_ref[0] * (cw[0:1] * cx2 + cw[1:2] * cx1 + cw[2:3] * cx)
    cvn = _rms(conv, gcv_ref[...])

    h1 = (x_ref[0] + _dot(sbn.astype(BF16), wout_ref[0:SB_WIDTH, :])
          + _dot(cvn.astype(BF16), wout_ref[SB_WIDTH:, :]))

    n2 = _rms(h1, gx_ref[...]).astype(BF16)
    qm = _dot(n2, wq_ref[...]) * (MEM_HEAD_DIM ** -0.5)
    heads = []
    for h in range(MEM_HEADS):
        sl = slice(h * MEM_HEAD_DIM, (h + 1) * MEM_HEAD_DIM)
        sc = _dot_nt(qm[:, sl].astype(BF16), km_ref[0, :, sl])
        p = jnp.exp(sc - jnp.max(sc, axis=-1, keepdims=True))
        p = p / jnp.sum(p, axis=-1, keepdims=True)
        heads.append(_dot(p.astype(BF16), vm_ref[0, :, sl]).astype(BF16))
    h2 = h1 + _dot(jnp.concatenate(heads, axis=1), wo_ref[...])
    h2_ref[0] = h2
    n3_ref[0] = _rms(h2, gffn_ref[...]).astype(BF16)


def _post_mixer(sbT, gb, cx, x, kmem, vmem, conv_w, g_sb, g_cv, w_out, g_x, w_q, w_o, g_ffn):
    b, s, d = x.shape
    nt = s // TOK_TILE
    m = kmem.shape[1]
    hb = TOK_TILE // SUBLANES
    const = lambda shape: pl.BlockSpec(shape, lambda bi, i: (0,) * len(shape))
    return pl.pallas_call(
        _post_kernel,
        grid=(b, nt),
        in_specs=[
            pl.BlockSpec((1, SB_WIDTH, TOK_TILE), lambda bi, i: (bi, 0, i)),
            pl.BlockSpec((1, TOK_TILE, CONV_CH), lambda bi, i: (bi, i, 0)),
            pl.BlockSpec((1, TOK_TILE, CONV_CH), lambda bi, i: (bi, i, 0)),
            pl.BlockSpec((1, SUBLANES, CONV_CH), lambda bi, i: (bi, jnp.maximum(i * hb - 1, 0), 0)),
            pl.BlockSpec((1, TOK_TILE, d), lambda bi, i: (bi, i, 0)),
            pl.BlockSpec((1, m, d), lambda bi, i: (bi, 0, 0)),
            pl.BlockSpec((1, m, d), lambda bi, i: (bi, 0, 0)),
            const((CONV_K, CONV_CH)),
            const((1, SB_WIDTH)),
            const((1, CONV_CH)),
            const((d, d)),
            const((1, d)),
            const((d, d)),
            const((d, d)),
            const((1, d)),
        ],
        out_specs=[
            pl.BlockSpec((1, TOK_TILE, d), lambda bi, i: (bi, i, 0)),
            pl.BlockSpec((1, TOK_TILE, d), lambda bi, i: (bi, i, 0)),
        ],
        out_shape=[
            jax.ShapeDtypeStruct((b, s, d), F32),
            jax.ShapeDtypeStruct((b, s, d), BF16),
        ],
        compiler_params=pltpu.CompilerParams(vmem_limit_bytes=VMEM_LIMIT),
        name="post_mixer",
    )(sbT, gb, cx, cx, x, kmem, vmem, conv_w, g_sb.reshape(1, -1), g_cv.reshape(1, -1), w_out,
      g_x.reshape(1, d), w_q, w_o, g_ffn.reshape(1, d))


def _top16(s):
    nk, nt = s.shape
    key_id = lax.broadcasted_iota(jnp.int32, (nk, nt), 0).astype(F32)
    slot = lax.broadcasted_iota(jnp.int32, (PEER_TOPK, nt), 0)
    cur = s
    rank = jnp.full((nk, nt), float(PEER_TOPK), F32)
    top = jnp.zeros((PEER_TOPK, nt), F32)
    for r in range(PEER_TOPK):
        m = jnp.max(cur, axis=0, keepdims=True)
        first = jnp.min(jnp.where(cur == m, key_id, float(nk)), axis=0, keepdims=True)
        sel = key_id == first
        rank = jnp.where(sel, float(r), rank)
        cur = jnp.where(sel, -jnp.inf, cur)
        top = jnp.where(slot == r, m, top)
    return rank, top


def _staircase(top1, top2):
    nt = top1.shape[1]
    row = lax.broadcasted_iota(jnp.int32, (PEER_TOPK, nt), 0).astype(F32)
    cnt = jnp.zeros((PEER_TOPK, nt), F32)
    front = top1 + top2[0:1]
    for _ in range(PEER_TOPK):
        m = jnp.max(front, axis=0, keepdims=True)
        a = jnp.min(jnp.where(front == m, row, float(PEER_TOPK)), axis=0, keepdims=True)
        sel = row == a
        cnt = jnp.where(sel, cnt + 1.0, cnt)
        nxt = jnp.sum(jnp.where(sel, cnt, 0.0), axis=0, keepdims=True)
        val = jnp.sum(jnp.where(row == nxt, top2, 0.0), axis=0, keepdims=True)
        val = jnp.where(nxt >= float(PEER_TOPK), -jnp.inf, val)
        front = jnp.where(sel, top1 + val, front)
    return cnt


def _gelu(x):
    return 0.5 * x * (1.0 + lax.erf(x * (2.0 ** -0.5)))


def _peer_kernel(n3_ref, h2_ref, wqT_ref, sk_ref, u_ref, vT_ref, gf_ref, out_ref,
                 acc_ref, st_ref, coef_ref, qT_ref, s_ref, rb_ref, eb_ref, ci_ref, ea_ref):
    c = pl.program_id(1)
    nchunks = pl.num_programs(1)

    @pl.when(c == 0)
    def _prepare():
        acc_ref[...] = jnp.zeros_like(acc_ref)
        qT_ref[...] = _dot_nt(wqT_ref[...], n3_ref[...]).astype(BF16)
        for hc in range(2 * PEER_HEADS):
            s = _dot(sk_ref[hc], qT_ref[hc * PEER_HALF:(hc + 1) * PEER_HALF, :])
            for lc in range(PEER_LC):
                s_ref[hc, lc] = s[:, lc * LANES:(lc + 1) * LANES]

        def select(idx, carry):
            h = idx // PEER_LC
            lc = idx % PEER_LC
            s1 = s_ref[2 * h, lc]
            s2 = s_ref[2 * h + 1, lc]
            rank1, top1 = _top16(s1)
            rank2, top2 = _top16(s2)
            cnt = _staircase(top1, top2)
            m1 = top1[0:1]
            m2 = top2[0:1]
            e1 = jnp.exp(top1 - m1)
            e2 = jnp.exp(top2 - m2)
            pref = jnp.zeros_like(cnt)
            for b in range(PEER_TOPK):
                pref = pref + jnp.where(cnt > float(b), e2[b:b + 1], 0.0)
            z = jnp.sum(e1 * pref, axis=0, keepdims=True)
            ci = jnp.zeros_like(s1)
            for a in range(PEER_TOPK):
                ci = ci + jnp.where(rank1 == float(a), cnt[a:a + 1], 0.0)
            rb_ref[h, lc] = rank2
            eb_ref[h, lc] = jnp.exp(s2 - m2)
            ci_ref[h, lc] = ci
            ea_ref[h, lc] = jnp.exp(s1 - m1) / z
            return carry

        lax.fori_loop(0, PEER_HEADS * PEER_LC, select, 0)

    st_ref[...] = _dot_nt(u_ref[...], n3_ref[...])

    def row_body(il, carry):
        i = c * PEER_ROWS + il
        r0 = pl.multiple_of(il * PEER_KEYS, PEER_KEYS)
        for lc in range(PEER_LC):
            ls = slice(lc * LANES, (lc + 1) * LANES)
            act = _gelu(st_ref[pl.ds(r0, PEER_KEYS), ls])
            gate = jnp.zeros((PEER_KEYS, LANES), F32)
            for h in range(PEER_HEADS):
                keep = rb_ref[h, lc] < ci_ref[h, lc, pl.ds(i, 1), :]
                gate = gate + jnp.where(keep, eb_ref[h, lc] * ea_ref[h, lc, pl.ds(i, 1), :], 0.0)
            coef_ref[pl.ds(r0, PEER_KEYS), ls] = (gate * act).astype(BF16)
        return carry

    lax.fori_loop(0, PEER_ROWS, row_body, 0)
    acc_ref[...] += _dot(vT_ref[...], coef_ref[...])

    @pl.when(c == nchunks - 1)
    def _finish():
        out_ref[...] = _rms(h2_ref[...] + acc_ref[...].T, gf_ref[...])


def _peer(n3, h2, wqT, sk, u, vT, g_final):
    t, d = n3.shape
    nt = t // PEER_TOK
    nc = PEER_EXPERTS // PEER_ECHUNK
    tile4 = (PEER_HEADS, PEER_LC, PEER_KEYS, LANES)
    return pl.pallas_call(
        _peer_kernel,
        grid=(nt, nc),
        in_specs=[
            pl.BlockSpec((PEER_TOK, d), lambda i, c: (i, 0)),
            pl.BlockSpec((PEER_TOK, d), lambda i, c: (i, 0)),
            pl.BlockSpec((PEER_HEADS * PEER_QDIM, d), lambda i, c: (0, 0)),
            pl.BlockSpec((2 * PEER_HEADS, PEER_KEYS, PEER_HALF), lambda i, c: (0, 0, 0)),
            pl.BlockSpec((PEER_ECHUNK, d), lambda i, c: (c, 0)),
            pl.BlockSpec((d, PEER_ECHUNK), lambda i, c: (0, c)),
            pl.BlockSpec((1, d), lambda i, c: (0, 0)),
        ],
        out_specs=pl.BlockSpec((PEER_TOK, d), lambda i, c: (i, 0)),
        out_shape=jax.ShapeDtypeStruct((t, d), F32),
        scratch_shapes=[
            pltpu.VMEM((d, PEER_TOK), F32),
            pltpu.VMEM((PEER_ECHUNK, PEER_TOK), F32),
            pltpu.VMEM((PEER_ECHUNK, PEER_TOK), BF16),
            pltpu.VMEM((PEER_HEADS * PEER_QDIM, PEER_TOK), BF16),
            pltpu.VMEM((2 * PEER_HEADS, PEER_LC, PEER_KEYS, LANES), F32),
            pltpu.VMEM(tile4, F32),
            pltpu.VMEM(tile4, F32),
            pltpu.VMEM(tile4, F32),
            pltpu.VMEM(tile4, F32),
        ],
        compiler_params=pltpu.CompilerParams(vmem_limit_bytes=VMEM_LIMIT),
        name="peer",
    )(n3, h2, wqT, sk, u, vT, g_final.reshape(1, d))


def kernel(x, mem, g_mix, w_in, conv_w, g_sb_out, g_conv_out, w_out, g_xattn, g_mem, w_q_mem, w_kv_mem, w_o_mem, g_ffn, w_query, sub_keys, expert_u, expert_v, g_final):
    b, s, d = x.shape
    assert g_mix.shape[0] == 1, "the final rmsnorm is fused into the single layer's PEER kernel"
    l = 0
    qT, k, vT, gb, cx = _in_proj(x, g_mix[l], w_in[l].astype(BF16))
    sbT = _attention(qT, k, vT)
    kmem, vmem = _mem_kv(mem, g_mem[l], w_kv_mem[l].astype(BF16))
    h2, n3 = _post_mixer(sbT, gb, cx, x, kmem, vmem, conv_w[l], g_sb_out[l], g_conv_out[l],
                         w_out[l].astype(BF16), g_xattn[l], w_q_mem[l].astype(BF16),
                         w_o_mem[l].astype(BF16), g_ffn[l])
    out = _peer(n3.reshape(b * s, d), h2.reshape(b * s, d),
                w_query[l].T.astype(BF16),
                sub_keys[l].reshape(2 * PEER_HEADS, PEER_KEYS, PEER_HALF).astype(BF16),
                expert_u[l].astype(BF16), expert_v[l].T.astype(BF16), g_final)
    return out.reshape(b, s, d)
```

```python
import functools
import math

import jax
import jax.numpy as jnp
from jax import lax
from jax.experimental import pallas as pl
from jax.experimental.pallas import tpu as pltpu

F32 = jnp.float32
BF16 = jnp.bfloat16

D_MODEL = 1024
EPS = 1e-6
SB_HEADS = 8
SB_HEAD_DIM = 64
SB_WIDTH = SB_HEADS * SB_HEAD_DIM
CONV_CH = D_MODEL - SB_WIDTH
CONV_K = 3
IN_COLS = 3 * SB_WIDTH + 3 * CONV_CH
MEM_HEADS = 4
MEM_HEAD_DIM = D_MODEL // MEM_HEADS
PEER_HEADS = 8
PEER_KEYS = 128
PEER_EXPERTS = PEER_KEYS * PEER_KEYS
PEER_QDIM = 256
PEER_HALF = PEER_QDIM // 2
PEER_TOPK = 16

LANES = 128
SUBLANES = 8

TOK_TILE = 512
ATT_BLOCK = 256
HEADS_PER_STEP = LANES // SB_HEAD_DIM
PEER_TOK = 512
PEER_LC = PEER_TOK // LANES
PEER_ECHUNK = 1024
PEER_ROWS = PEER_ECHUNK // PEER_KEYS
PEER_SUB = 512
VMEM_LIMIT = 58 * 1024 * 1024
LOG2E = math.log2(math.e)
EXP2_CLAMP = 126.0


def _rms(x, g):
    return x * lax.rsqrt(jnp.mean(x * x, axis=-1, keepdims=True) + EPS) * g


def _dot(a, b):
    return jnp.dot(a, b, preferred_element_type=F32)


def _dot_nt(a, b):
    return lax.dot_general(a, b, (((1,), (1,)), ((), ())), preferred_element_type=F32)


def _inproj_kernel(x_ref, g_ref, w_ref, qT_ref, k_ref, vT_ref, gb_ref, cx_ref):
    n = _rms(x_ref[0], g_ref[...]).astype(BF16)
    proj = _dot(n, w_ref[...])
    q = proj[:, 0:SB_WIDTH] * (SB_HEAD_DIM ** -0.5 * LOG2E)
    qT_ref[0] = q.T.astype(BF16)
    k_ref[0] = proj[:, SB_WIDTH:2 * SB_WIDTH].astype(BF16)
    vT = proj[:, 2 * SB_WIDTH:3 * SB_WIDTH].T.astype(BF16)
    for j in range(TOK_TILE // ATT_BLOCK):
        vT_ref[0, j] = vT[:, j * ATT_BLOCK:(j + 1) * ATT_BLOCK]
    o = 3 * SB_WIDTH
    gb_ref[0] = proj[:, o:o + CONV_CH]
    cx_ref[0] = proj[:, o + CONV_CH:o + 2 * CONV_CH] * proj[:, o + 2 * CONV_CH:o + 3 * CONV_CH]


def _in_proj(x, g_mix, w_in):
    b, s, d = x.shape
    nt = s // TOK_TILE
    return pl.pallas_call(
        _inproj_kernel,
        grid=(b, nt),
        in_specs=[
            pl.BlockSpec((1, TOK_TILE, d), lambda bi, i: (bi, i, 0)),
            pl.BlockSpec((1, d), lambda bi, i: (0, 0)),
            pl.BlockSpec((d, IN_COLS), lambda bi, i: (0, 0)),
        ],
        out_specs=[
            pl.BlockSpec((1, SB_WIDTH, TOK_TILE), lambda bi, i: (bi, 0, i)),
            pl.BlockSpec((1, TOK_TILE, SB_WIDTH), lambda bi, i: (bi, i, 0)),
            pl.BlockSpec((1, TOK_TILE // ATT_BLOCK, SB_WIDTH, ATT_BLOCK), lambda bi, i: (bi, i, 0, 0)),
            pl.BlockSpec((1, TOK_TILE, CONV_CH), lambda bi, i: (bi, i, 0)),
            pl.BlockSpec((1, TOK_TILE, CONV_CH), lambda bi, i: (bi, i, 0)),
        ],
        out_shape=[
            jax.ShapeDtypeStruct((b, SB_WIDTH, s), BF16),
            jax.ShapeDtypeStruct((b, s, SB_WIDTH), BF16),
            jax.ShapeDtypeStruct((b, s // ATT_BLOCK, SB_WIDTH, ATT_BLOCK), BF16),
            jax.ShapeDtypeStruct((b, s, CONV_CH), F32),
            jax.ShapeDtypeStruct((b, s, CONV_CH), F32),
        ],
        compiler_params=pltpu.CompilerParams(vmem_limit_bytes=VMEM_LIMIT),
        name="in_proj",
    )(x, g_mix.reshape(1, d), w_in)


def _attn_kernel(qT_ref, k_ref, vT_ref, u_ref, o_ref, qm_ref, acc_ref, c_ref, s_ref,
                 z_ref, n_ref, a_ref):
    qi = pl.program_id(1)
    tb = ATT_BLOCK
    feat = lax.broadcasted_iota(jnp.int32, (LANES, tb), 0)
    for h in range(SB_HEADS):
        p, hh = divmod(h, HEADS_PER_STEP)
        slab = qT_ref[0, p * LANES:(p + 1) * LANES, :]
        in_head = (feat >= hh * SB_HEAD_DIM) & (feat < (hh + 1) * SB_HEAD_DIM)
        qm_ref[h] = jnp.where(in_head, slab, jnp.zeros_like(slab))
    acc_ref[...] = jnp.zeros_like(acc_ref)
    c_ref[...] = jnp.zeros_like(c_ref)
    key_pos = lax.broadcasted_iota(jnp.int32, (tb, tb), 0)
    qry_pos = lax.broadcasted_iota(jnp.int32, (tb, tb), 1)
    past = key_pos < qry_pos

    def block(jb, diagonal):
        r0 = pl.multiple_of(jb * tb, tb)
        for p in range(SB_WIDTH // LANES):
            kb = k_ref[0, pl.ds(r0, tb), p * LANES:(p + 1) * LANES]
            for hh in range(HEADS_PER_STEP):
                h = p * HEADS_PER_STEP + hh
                z = _dot(kb, qm_ref[h])
                n = jnp.maximum(z, jnp.log(1.0 + jnp.exp2(jnp.minimum(z, EXP2_CLAMP))) * LOG2E)
                if diagonal:
                    n = jnp.where(past, n, 0.0)
                z_ref[h] = z
                n_ref[h] = n.astype(BF16)
        for h in range(SB_HEADS):
            cum = _dot(u_ref[...], n_ref[h])
            a = jnp.exp2(z_ref[h] - cum)
            if diagonal:
                a = jnp.where(past, a, 0.0)
            a_ref[h] = a.astype(BF16)
            s_ref[h:h + 1, :] = cum[0:1, :]
        for h in range(SB_HEADS):
            vb = vT_ref[0, jb, h * SB_HEAD_DIM:(h + 1) * SB_HEAD_DIM, :]
            c = c_ref[h:h + 1, :]
            acc_ref[h] += jnp.exp2(-c) * _dot(vb, a_ref[h])
            c_ref[h:h + 1, :] = c + s_ref[h:h + 1, :]

    block(qi, True)

    def body(j, carry):
        block(qi - 1 - j, False)
        return carry

    lax.fori_loop(0, qi, body, 0)
    for h in range(SB_HEADS):
        o_ref[0, h * SB_HEAD_DIM:(h + 1) * SB_HEAD_DIM, :] = acc_ref[h]


def _attention(qT, k, vT):
    b, _, s = qT.shape
    nq = s // ATT_BLOCK
    tri = (lax.broadcasted_iota(jnp.int32, (ATT_BLOCK, ATT_BLOCK), 1)
           >= lax.broadcasted_iota(jnp.int32, (ATT_BLOCK, ATT_BLOCK), 0)).astype(BF16)
    return pl.pallas_call(
        _attn_kernel,
        grid=(b, nq),
        in_specs=[
            pl.BlockSpec((1, SB_WIDTH, ATT_BLOCK), lambda bi, q: (bi, 0, q)),
            pl.BlockSpec((1, s, SB_WIDTH), lambda bi, q: (bi, 0, 0)),
            pl.BlockSpec((1, nq, SB_WIDTH, ATT_BLOCK), lambda bi, q: (bi, 0, 0, 0)),
            pl.BlockSpec((ATT_BLOCK, ATT_BLOCK), lambda bi, q: (0, 0)),
        ],
        out_specs=pl.BlockSpec((1, SB_WIDTH, ATT_BLOCK), lambda bi, q: (bi, 0, q)),
        out_shape=jax.ShapeDtypeStruct((b, SB_WIDTH, s), F32),
        scratch_shapes=[
            pltpu.VMEM((SB_HEADS, LANES, ATT_BLOCK), BF16),
            pltpu.VMEM((SB_HEADS, SB_HEAD_DIM, ATT_BLOCK), F32),
            pltpu.VMEM((SB_HEADS, ATT_BLOCK), F32),
            pltpu.VMEM((SB_HEADS, ATT_BLOCK), F32),
            pltpu.VMEM((SB_HEADS, ATT_BLOCK, ATT_BLOCK), F32),
            pltpu.VMEM((SB_HEADS, ATT_BLOCK, ATT_BLOCK), BF16),
            pltpu.VMEM((SB_HEADS, ATT_BLOCK, ATT_BLOCK), BF16),
        ],
        compiler_params=pltpu.CompilerParams(vmem_limit_bytes=VMEM_LIMIT),
        name="sb_attention",
    )(qT, k, vT, tri)


def _memkv_kernel(mem_ref, g_ref, w_ref, k_ref, v_ref):
    mn = _rms(mem_ref[0], g_ref[...]).astype(BF16)
    kv = _dot(mn, w_ref[...])
    k_ref[0] = kv[:, :D_MODEL].astype(BF16)
    v_ref[0] = kv[:, D_MODEL:].astype(BF16)


def _mem_kv(mem, g_mem, w_kv):
    b, m, d = mem.shape
    return pl.pallas_call(
        _memkv_kernel,
        grid=(b,),
        in_specs=[
            pl.BlockSpec((1, m, d), lambda bi: (bi, 0, 0)),
            pl.BlockSpec((1, d), lambda bi: (0, 0)),
            pl.BlockSpec((d, 2 * d), lambda bi: (0, 0)),
        ],
        out_specs=[pl.BlockSpec((1, m, d), lambda bi: (bi, 0, 0))] * 2,
        out_shape=[jax.ShapeDtypeStruct((b, m, d), BF16)] * 2,
        compiler_params=pltpu.CompilerParams(vmem_limit_bytes=VMEM_LIMIT),
        name="mem_kv",
    )(mem, g_mem.reshape(1, d), w_kv)


def _post_kernel(sbT_ref, gb_ref, cx_ref, halo_ref, x_ref, km_ref, vm_ref, cw_ref,
                 gsb_ref, gcv_ref, wout_ref, gx_ref, wq_ref, wo_ref, gffn_ref,
                 h2_ref, n3_ref):
    i = pl.program_id(1)
    tm = TOK_TILE
    sbn = _rms(sbT_ref[0].T, gsb_ref[...])

    cx = cx_ref[0]
    halo = jnp.where(i > 0, halo_ref[0], 0.0)
    row = lax.broadcasted_iota(jnp.int32, (tm, CONV_CH), 0)
    cx1 = jnp.where(row == 0, halo[7:8], pltpu.roll(cx, 1, 0))
    cx2 = jnp.where(row == 0, halo[6:7],
                    jnp.where(row == 1, halo[7:8], pltpu.roll(cx, 2, 0)))
    cw = cw_ref[...]
    conv = gb_ref[0] * (cw[0:1] * cx2 + cw[1:2] * cx1 + cw[2:3] * cx)
    cvn = _rms(conv, gcv_ref[...])

    h1 = (x_ref[0] + _dot(sbn.astype(BF16), wout_ref[0:SB_WIDTH, :])
          + _dot(cvn.astype(BF16), wout_ref[SB_WIDTH:, :]))

    n2 = _rms(h1, gx_ref[...]).astype(BF16)
    qm = _dot(n2, wq_ref[...]) * (MEM_HEAD_DIM ** -0.5)
    heads = []
    for h in range(MEM_HEADS):
        sl = slice(h * MEM_HEAD_DIM, (h + 1) * MEM_HEAD_DIM)
        sc = _dot_nt(qm[:, sl].astype(BF16), km_ref[0, :, sl])
        p = jnp.exp(sc - jnp.max(sc, axis=-1, keepdims=True))
        p = p / jnp.sum(p, axis=-1, keepdims=True)
        heads.append(_dot(p.astype(BF16), vm_ref[0, :, sl]).astype(BF16))
    h2 = h1 + _dot(jnp.concatenate(heads, axis=1), wo_ref[...])
    h2_ref[0] = h2
    n3_ref[0] = _rms(h2, gffn_ref[...]).astype(BF16)


def _post_mixer(sbT, gb, cx, x, kmem, vmem, conv_w, g_sb, g_cv, w_out, g_x, w_q, w_o, g_ffn):
    b, s, d = x.shape
    nt = s // TOK_TILE
    m = kmem.shape[1]
    hb = TOK_TILE // SUBLANES
    const = lambda shape: pl.BlockSpec(shape, lambda bi, i: (0,) * len(shape))
    return pl.pallas_call(
        _post_kernel,
        grid=(b, nt),
        in_specs=[
            pl.BlockSpec((1, SB_WIDTH, TOK_TILE), lambda bi, i: (bi, 0, i)),
            pl.BlockSpec((1, TOK_TILE, CONV_CH), lambda bi, i: (bi, i, 0)),
            pl.BlockSpec((1, TOK_TILE, CONV_CH), lambda bi, i: (bi, i, 0)),
            pl.BlockSpec((1, SUBLANES, CONV_CH), lambda bi, i: (bi, jnp.maximum(i * hb - 1, 0), 0)),
            pl.BlockSpec((1, TOK_TILE, d), lambda bi, i: (bi, i, 0)),
            pl.BlockSpec((1, m, d), lambda bi, i: (bi, 0, 0)),
            pl.BlockSpec((1, m, d), lambda bi, i: (bi, 0, 0)),
            const((CONV_K, CONV_CH)),
            const((1, SB_WIDTH)),
            const((1, CONV_CH)),
            const((d, d)),
            const((1, d)),
            const((d, d)),
            const((d, d)),
            const((1, d)),
        ],
        out_specs=[
            pl.BlockSpec((1, TOK_TILE, d), lambda bi, i: (bi, i, 0)),
            pl.BlockSpec((1, TOK_TILE, d), lambda bi, i: (bi, i, 0)),
        ],
        out_shape=[
            jax.ShapeDtypeStruct((b, s, d), F32),
            jax.ShapeDtypeStruct((b, s, d), BF16),
        ],
        compiler_params=pltpu.CompilerParams(vmem_limit_bytes=VMEM_LIMIT),
        name="post_mixer",
    )(sbT, gb, cx, cx, x, kmem, vmem, conv_w, g_sb.reshape(1, -1), g_cv.reshape(1, -1), w_out,
      g_x.reshape(1, d), w_q, w_o, g_ffn.reshape(1, d))


def _top16(s):
    nk, nt = s.shape
    key_id = lax.broadcasted_iota(jnp.int32, (nk, nt), 0).astype(F32)
    slot = lax.broadcasted_iota(jnp.int32, (PEER_TOPK, nt), 0)
    cur = s
    rank = jnp.full((nk, nt), float(PEER_TOPK), F32)
    top = jnp.zeros((PEER_TOPK, nt), F32)
    for r in range(PEER_TOPK):
        m = jnp.max(cur, axis=0, keepdims=True)
        first = jnp.min(jnp.where(cur == m, key_id, float(nk)), axis=0, keepdims=True)
        sel = key_id == first
        rank = jnp.where(sel, float(r), rank)
        cur = jnp.where(sel, -jnp.inf, cur)
        top = jnp.where(slot == r, m, top)
    return rank, top


def _top16_distinct(s, want_rank):
    nk, nt = s.shape
    slot = lax.broadcasted_iota(jnp.int32, (PEER_TOPK, nt), 0)
    cur = s
    rank = jnp.full((nk, nt), float(PEER_TOPK), F32)
    top = jnp.zeros((PEER_TOPK, nt), F32)
    for r in range(PEER_TOPK):
        m = jnp.max(cur, axis=0, keepdims=True)
        hit = cur == m
        if want_rank:
            rank = jnp.where(hit, float(r), rank)
        cur = jnp.where(hit, -jnp.inf, cur)
        top = jnp.where(slot == r, m, top)
    taken = jnp.sum(jnp.where(cur == -jnp.inf, 1.0, 0.0), axis=0, keepdims=True)
    return rank, top, jnp.abs(taken - float(PEER_TOPK))


def _staircase(top1, top2):
    nt = top1.shape[1]
    row = lax.broadcasted_iota(jnp.int32, (PEER_TOPK, nt), 0).astype(F32)
    cnt = jnp.zeros((PEER_TOPK, nt), F32)
    front = top1 + top2[0:1]
    for _ in range(PEER_TOPK):
        m = jnp.max(front, axis=0, keepdims=True)
        a = jnp.min(jnp.where(front == m, row, float(PEER_TOPK)), axis=0, keepdims=True)
        sel = row == a
        cnt = jnp.where(sel, cnt + 1.0, cnt)
        nxt = jnp.sum(jnp.where(sel, cnt, 0.0), axis=0, keepdims=True)
        val = jnp.sum(jnp.where(row == nxt, top2, 0.0), axis=0, keepdims=True)
        val = jnp.where(nxt >= float(PEER_TOPK), -jnp.inf, val)
        front = jnp.where(sel, top1 + val, front)
    return cnt


def _gelu(x):
    return 0.5 * x * (1.0 + lax.erf(x * (2.0 ** -0.5)))


def _peer_kernel(n3_ref, h2_ref, wqT_ref, sk_ref, u_ref, vT_ref, gf_ref, out_ref,
                 acc_ref, st_ref, coef_ref, qT_ref, s_ref, rb_ref, eb_ref, ci_ref, ea_ref):
    c = pl.program_id(1)
    nchunks = pl.num_programs(1)

    @pl.when(c == 0)
    def _prepare():
        acc_ref[...] = jnp.zeros_like(acc_ref)
        qT_ref[...] = _dot_nt(wqT_ref[...], n3_ref[...]).astype(BF16)
        for hc in range(2 * PEER_HEADS):
            s = _dot(sk_ref[hc], qT_ref[hc * PEER_HALF:(hc + 1) * PEER_HALF, :])
            for lc in range(PEER_LC):
                s_ref[hc, lc] = s[:, lc * LANES:(lc + 1) * LANES]

        def select(h, lc, exact):
            s1 = s_ref[2 * h, lc]
            s2 = s_ref[2 * h + 1, lc]
            if exact:
                rank1, top1 = _top16(s1)
                rank2, top2 = _top16(s2)
                tie = None
            else:
                _, top1, tie1 = _top16_distinct(s1, False)
                rank2, top2, tie2 = _top16_distinct(s2, True)
                tie = jnp.maximum(tie1, tie2)
            cnt = _staircase(top1, top2)
            m1 = top1[0:1]
            m2 = top2[0:1]
            e1 = jnp.exp(top1 - m1)
            e2 = jnp.exp(top2 - m2)
            pref = jnp.zeros_like(cnt)
            for b in range(PEER_TOPK):
                pref = pref + jnp.where(cnt > float(b), e2[b:b + 1], 0.0)
            z = jnp.sum(e1 * pref, axis=0, keepdims=True)
            ci = jnp.zeros_like(s1)
            for a in range(PEER_TOPK):
                in_row = (rank1 == float(a)) if exact else (s1 == top1[a:a + 1])
                ci = jnp.where(in_row, cnt[a:a + 1], ci)
            rb_ref[h, lc] = rank2
            eb_ref[h, lc] = jnp.exp(s2 - m2)
            ci_ref[h, lc] = ci
            ea_ref[h, lc] = jnp.exp(s1 - m1) / z
            return tie

        def fast(h, tie):
            for lc in range(PEER_LC):
                tie = jnp.maximum(tie, select(h, lc, False))
            return tie

        tie = lax.fori_loop(0, PEER_HEADS, fast, jnp.zeros((1, LANES), F32))

        @pl.when(jnp.max(tie) > 0.0)
        def _with_ties():
            def slow(idx, carry):
                select(idx // PEER_LC, idx % PEER_LC, True)
                return carry

            lax.fori_loop(0, PEER_HEADS * PEER_LC, slow, 0)

    nsub = PEER_ECHUNK // PEER_SUB

    def pre_activations(sub):
        rows = slice(sub * PEER_SUB, (sub + 1) * PEER_SUB)
        st_ref[rows, :] = _dot_nt(u_ref[rows, :], n3_ref[...])

    pre_activations(0)
    for sub in range(nsub):
        rows = slice(sub * PEER_SUB, (sub + 1) * PEER_SUB)
        if sub + 1 < nsub:
            pre_activations(sub + 1)
        for il in range(sub * PEER_SUB // PEER_KEYS, (sub + 1) * PEER_SUB // PEER_KEYS):
            i = c * PEER_ROWS + il
            rs = slice(il * PEER_KEYS, (il + 1) * PEER_KEYS)
            for lc in range(PEER_LC):
                ls = slice(lc * LANES, (lc + 1) * LANES)
                act = _gelu(st_ref[rs, ls])
                gate = jnp.zeros((PEER_KEYS, LANES), F32)
                for h in range(PEER_HEADS):
                    keep = rb_ref[h, lc] < ci_ref[h, lc, pl.ds(i, 1), :]
                    gate = gate + jnp.where(keep, eb_ref[h, lc] * ea_ref[h, lc, pl.ds(i, 1), :], 0.0)
                coef_ref[rs, ls] = (gate * act).astype(BF16)
        acc_ref[...] += _dot(vT_ref[:, rows], coef_ref[rows, :])

    @pl.when(c == nchunks - 1)
    def _finish():
        out_ref[...] = _rms(h2_ref[...] + acc_ref[...].T, gf_ref[...])


def _peer(n3, h2, wqT, sk, u, vT, g_final):
    t, d = n3.shape
    nt = t // PEER_TOK
    nc = PEER_EXPERTS // PEER_ECHUNK
    tile4 = (PEER_HEADS, PEER_LC, PEER_KEYS, LANES)
    return pl.pallas_call(
        _peer_kernel,
        grid=(nt, nc),
        in_specs=[
            pl.BlockSpec((PEER_TOK, d), lambda i, c: (i, 0)),
            pl.BlockSpec((PEER_TOK, d), lambda i, c: (i, 0)),
            pl.BlockSpec((PEER_HEADS * PEER_QDIM, d), lambda i, c: (0, 0)),
            pl.BlockSpec((2 * PEER_HEADS, PEER_KEYS, PEER_HALF), lambda i, c: (0, 0, 0)),
            pl.BlockSpec((PEER_ECHUNK, d), lambda i, c: (c, 0)),
            pl.BlockSpec((d, PEER_ECHUNK), lambda i, c: (0, c)),
            pl.BlockSpec((1, d), lambda i, c: (0, 0)),
        ],
        out_specs=pl.BlockSpec((PEER_TOK, d), lambda i, c: (i, 0)),
        out_shape=jax.ShapeDtypeStruct((t, d), F32),
        scratch_shapes=[
            pltpu.VMEM((d, PEER_TOK), F32),
            pltpu.VMEM((PEER_ECHUNK, PEER_TOK), F32),
            pltpu.VMEM((PEER_ECHUNK, PEER_TOK), BF16),
            pltpu.VMEM((PEER_HEADS * PEER_QDIM, PEER_TOK), BF16),
            pltpu.VMEM((2 * PEER_HEADS, PEER_LC, PEER_KEYS, LANES), F32),
            pltpu.VMEM(tile4, F32),
            pltpu.VMEM(tile4, F32),
            pltpu.VMEM(tile4, F32),
            pltpu.VMEM(tile4, F32),
        ],
        compiler_params=pltpu.CompilerParams(vmem_limit_bytes=VMEM_LIMIT),
        name="peer",
    )(n3, h2, wqT, sk, u, vT, g_final.reshape(1, d))


def kernel(x, mem, g_mix, w_in, conv_w, g_sb_out, g_conv_out, w_out, g_xattn, g_mem, w_q_mem, w_kv_mem, w_o_mem, g_ffn, w_query, sub_keys, expert_u, expert_v, g_final):
    b, s, d = x.shape
    assert g_mix.shape[0] == 1, "the final rmsnorm is fused into the single layer's PEER kernel"
    l = 0
    qT, k, vT, gb, cx = _in_proj(x, g_mix[l], w_in[l].astype(BF16))
    sbT = _attention(qT, k, vT)
    kmem, vmem = _mem_kv(mem, g_mem[l], w_kv_mem[l].astype(BF16))
    h2, n3 = _post_mixer(sbT, gb, cx, x, kmem, vmem, conv_w[l], g_sb_out[l], g_conv_out[l],
                         w_out[l].astype(BF16), g_xattn[l], w_q_mem[l].astype(BF16),
                         w_o_mem[l].astype(BF16), g_ffn[l])
    out = _peer(n3.reshape(b * s, d), h2.reshape(b * s, d),
                w_query[l].T.astype(BF16),
                sub_keys[l].reshape(2 * PEER_HEADS, PEER_KEYS, PEER_HALF).astype(BF16),
                expert_u[l].astype(BF16), expert_v[l].T.astype(BF16), g_final)
    return out.reshape(b, s, d)
```

```python
import functools
import math

import jax
import jax.numpy as jnp
from jax import lax
from jax.experimental import pallas as pl
from jax.experimental.pallas import tpu as pltpu

F32 = jnp.float32
BF16 = jnp.bfloat16

D_MODEL = 1024
EPS = 1e-6
SB_HEADS = 8
SB_HEAD_DIM = 64
SB_WIDTH = SB_HEADS * SB_HEAD_DIM
CONV_CH = D_MODEL - SB_WIDTH
CONV_K = 3
IN_COLS = 3 * SB_WIDTH + 3 * CONV_CH
MEM_HEADS = 4
MEM_HEAD_DIM = D_MODEL // MEM_HEADS
PEER_HEADS = 8
PEER_KEYS = 128
PEER_EXPERTS = PEER_KEYS * PEER_KEYS
PEER_QDIM = 256
PEER_HALF = PEER_QDIM // 2
PEER_TOPK = 16

LANES = 128
SUBLANES = 8

TOK_TILE = 512
ATT_BLOCK = 256
HEADS_PER_STEP = LANES // SB_HEAD_DIM
PEER_TOK = 512
PEER_LC = PEER_TOK // LANES
PEER_ECHUNK = 1024
PEER_ROWS = PEER_ECHUNK // PEER_KEYS
PEER_SUB = 512
PEER_KSLAB = 64
VMEM_LIMIT = 58 * 1024 * 1024
LOG2E = math.log2(math.e)
EXP2_CLAMP = 126.0


def _rms(x, g):
    return x * lax.rsqrt(jnp.mean(x * x, axis=-1, keepdims=True) + EPS) * g


def _dot(a, b):
    return jnp.dot(a, b, preferred_element_type=F32)


def _dot_nt(a, b):
    return lax.dot_general(a, b, (((1,), (1,)), ((), ())), preferred_element_type=F32)


def _inproj_kernel(x_ref, g_ref, w_ref, qT_ref, k_ref, vT_ref, gb_ref, cx_ref):
    n = _rms(x_ref[0], g_ref[...]).astype(BF16)
    proj = _dot(n, w_ref[...])
    q = proj[:, 0:SB_WIDTH] * (SB_HEAD_DIM ** -0.5 * LOG2E)
    qT_ref[0] = q.T.astype(BF16)
    k_ref[0] = proj[:, SB_WIDTH:2 * SB_WIDTH].astype(BF16)
    vT = proj[:, 2 * SB_WIDTH:3 * SB_WIDTH].T.astype(BF16)
    for j in range(TOK_TILE // ATT_BLOCK):
        vT_ref[0, j] = vT[:, j * ATT_BLOCK:(j + 1) * ATT_BLOCK]
    o = 3 * SB_WIDTH
    gb_ref[0] = proj[:, o:o + CONV_CH]
    cx_ref[0] = proj[:, o + CONV_CH:o + 2 * CONV_CH] * proj[:, o + 2 * CONV_CH:o + 3 * CONV_CH]


def _in_proj(x, g_mix, w_in):
    b, s, d = x.shape
    nt = s // TOK_TILE
    return pl.pallas_call(
        _inproj_kernel,
        grid=(b, nt),
        in_specs=[
            pl.BlockSpec((1, TOK_TILE, d), lambda bi, i: (bi, i, 0)),
            pl.BlockSpec((1, d), lambda bi, i: (0, 0)),
            pl.BlockSpec((d, IN_COLS), lambda bi, i: (0, 0)),
        ],
        out_specs=[
            pl.BlockSpec((1, SB_WIDTH, TOK_TILE), lambda bi, i: (bi, 0, i)),
            pl.BlockSpec((1, TOK_TILE, SB_WIDTH), lambda bi, i: (bi, i, 0)),
            pl.BlockSpec((1, TOK_TILE // ATT_BLOCK, SB_WIDTH, ATT_BLOCK), lambda bi, i: (bi, i, 0, 0)),
            pl.BlockSpec((1, TOK_TILE, CONV_CH), lambda bi, i: (bi, i, 0)),
            pl.BlockSpec((1, TOK_TILE, CONV_CH), lambda bi, i: (bi, i, 0)),
        ],
        out_shape=[
            jax.ShapeDtypeStruct((b, SB_WIDTH, s), BF16),
            jax.ShapeDtypeStruct((b, s, SB_WIDTH), BF16),
            jax.ShapeDtypeStruct((b, s // ATT_BLOCK, SB_WIDTH, ATT_BLOCK), BF16),
            jax.ShapeDtypeStruct((b, s, CONV_CH), F32),
            jax.ShapeDtypeStruct((b, s, CONV_CH), F32),
        ],
        compiler_params=pltpu.CompilerParams(vmem_limit_bytes=VMEM_LIMIT),
        name="in_proj",
    )(x, g_mix.reshape(1, d), w_in)


def _attn_kernel(qT_ref, k_ref, vT_ref, u_ref, o_ref, qm_ref, acc_ref, c_ref, s_ref,
                 z_ref, n_ref, a_ref):
    qi = pl.program_id(1)
    tb = ATT_BLOCK
    feat = lax.broadcasted_iota(jnp.int32, (LANES, tb), 0)
    for h in range(SB_HEADS):
        p, hh = divmod(h, HEADS_PER_STEP)
        slab = qT_ref[0, p * LANES:(p + 1) * LANES, :]
        in_head = (feat >= hh * SB_HEAD_DIM) & (feat < (hh + 1) * SB_HEAD_DIM)
        qm_ref[h] = jnp.where(in_head, slab, jnp.zeros_like(slab))
    acc_ref[...] = jnp.zeros_like(acc_ref)
    c_ref[...] = jnp.zeros_like(c_ref)
    key_pos = lax.broadcasted_iota(jnp.int32, (tb, tb), 0)
    qry_pos = lax.broadcasted_iota(jnp.int32, (tb, tb), 1)
    past = key_pos < qry_pos

    def block(jb, diagonal):
        r0 = pl.multiple_of(jb * tb, tb)
        for p in range(SB_WIDTH // LANES):
            kb = k_ref[0, pl.ds(r0, tb), p * LANES:(p + 1) * LANES]
            for hh in range(HEADS_PER_STEP):
                h = p * HEADS_PER_STEP + hh
                z = _dot(kb, qm_ref[h])
                n = jnp.maximum(z, jnp.log(1.0 + jnp.exp2(jnp.minimum(z, EXP2_CLAMP))) * LOG2E)
                if diagonal:
                    n = jnp.where(past, n, 0.0)
                z_ref[h] = z
                n_ref[h] = n.astype(BF16)
        for h in range(SB_HEADS):
            cum = _dot(u_ref[...], n_ref[h])
            a = jnp.exp2(z_ref[h] - cum)
            if diagonal:
                a = jnp.where(past, a, 0.0)
            a_ref[h] = a.astype(BF16)
            s_ref[h:h + 1, :] = cum[0:1, :]
        for h in range(SB_HEADS):
            vb = vT_ref[0, jb, h * SB_HEAD_DIM:(h + 1) * SB_HEAD_DIM, :]
            c = c_ref[h:h + 1, :]
            acc_ref[h] += jnp.exp2(-c) * _dot(vb, a_ref[h])
            c_ref[h:h + 1, :] = c + s_ref[h:h + 1, :]

    block(qi, True)

    def body(j, carry):
        block(qi - 1 - j, False)
        return carry

    lax.fori_loop(0, qi, body, 0)
    for h in range(SB_HEADS):
        o_ref[0, h * SB_HEAD_DIM:(h + 1) * SB_HEAD_DIM, :] = acc_ref[h]


def _attention(qT, k, vT):
    b, _, s = qT.shape
    nq = s // ATT_BLOCK
    tri = (lax.broadcasted_iota(jnp.int32, (ATT_BLOCK, ATT_BLOCK), 1)
           >= lax.broadcasted_iota(jnp.int32, (ATT_BLOCK, ATT_BLOCK), 0)).astype(BF16)
    return pl.pallas_call(
        _attn_kernel,
        grid=(b, nq),
        in_specs=[
            pl.BlockSpec((1, SB_WIDTH, ATT_BLOCK), lambda bi, q: (bi, 0, q)),
            pl.BlockSpec((1, s, SB_WIDTH), lambda bi, q: (bi, 0, 0)),
            pl.BlockSpec((1, nq, SB_WIDTH, ATT_BLOCK), lambda bi, q: (bi, 0, 0, 0)),
            pl.BlockSpec((ATT_BLOCK, ATT_BLOCK), lambda bi, q: (0, 0)),
        ],
        out_specs=pl.BlockSpec((1, SB_WIDTH, ATT_BLOCK), lambda bi, q: (bi, 0, q)),
        out_shape=jax.ShapeDtypeStruct((b, SB_WIDTH, s), F32),
        scratch_shapes=[
            pltpu.VMEM((SB_HEADS, LANES, ATT_BLOCK), BF16),
            pltpu.VMEM((SB_HEADS, SB_HEAD_DIM, ATT_BLOCK), F32),
            pltpu.VMEM((SB_HEADS, ATT_BLOCK), F32),
            pltpu.VMEM((SB_HEADS, ATT_BLOCK), F32),
            pltpu.VMEM((SB_HEADS, ATT_BLOCK, ATT_BLOCK), F32),
            pltpu.VMEM((SB_HEADS, ATT_BLOCK, ATT_BLOCK), BF16),
            pltpu.VMEM((SB_HEADS, ATT_BLOCK, ATT_BLOCK), BF16),
        ],
        compiler_params=pltpu.CompilerParams(vmem_limit_bytes=VMEM_LIMIT),
        name="sb_attention",
    )(qT, k, vT, tri)


def _memkv_kernel(mem_ref, g_ref, w_ref, k_ref, v_ref):
    mn = _rms(mem_ref[0], g_ref[...]).astype(BF16)
    kv = _dot(mn, w_ref[...])
    k_ref[0] = kv[:, :D_MODEL].astype(BF16)
    v_ref[0] = kv[:, D_MODEL:].astype(BF16)


def _mem_kv(mem, g_mem, w_kv):
    b, m, d = mem.shape
    return pl.pallas_call(
        _memkv_kernel,
        grid=(b,),
        in_specs=[
            pl.BlockSpec((1, m, d), lambda bi: (bi, 0, 0)),
            pl.BlockSpec((1, d), lambda bi: (0, 0)),
            pl.BlockSpec((d, 2 * d), lambda bi: (0, 0)),
        ],
        out_specs=[pl.BlockSpec((1, m, d), lambda bi: (bi, 0, 0))] * 2,
        out_shape=[jax.ShapeDtypeStruct((b, m, d), BF16)] * 2,
        compiler_params=pltpu.CompilerParams(vmem_limit_bytes=VMEM_LIMIT),
        name="mem_kv",
    )(mem, g_mem.reshape(1, d), w_kv)


def _post_kernel(sbT_ref, gb_ref, cx_ref, halo_ref, x_ref, km_ref, vm_ref, cw_ref,
                 gsb_ref, gcv_ref, wout_ref, gx_ref, wq_ref, wo_ref, gffn_ref,
                 h2_ref, n3_ref):
    i = pl.program_id(1)
    tm = TOK_TILE
    sbn = _rms(sbT_ref[0].T, gsb_ref[...])

    cx = cx_ref[0]
    halo = jnp.where(i > 0, halo_ref[0], 0.0)
    row = lax.broadcasted_iota(jnp.int32, (tm, CONV_CH), 0)
    cx1 = jnp.where(row == 0, halo[7:8], pltpu.roll(cx, 1, 0))
    cx2 = jnp.where(row == 0, halo[6:7],
                    jnp.where(row == 1, halo[7:8], pltpu.roll(cx, 2, 0)))
    cw = cw_ref[...]
    conv = gb_ref[0] * (cw[0:1] * cx2 + cw[1:2] * cx1 + cw[2:3] * cx)
    cvn = _rms(conv, gcv_ref[...])

    h1 = (x_ref[0] + _dot(sbn.astype(BF16), wout_ref[0:SB_WIDTH, :])
          + _dot(cvn.astype(BF16), wout_ref[SB_WIDTH:, :]))

    n2 = _rms(h1, gx_ref[...]).astype(BF16)
    qm = _dot(n2, wq_ref[...]) * (MEM_HEAD_DIM ** -0.5)
    heads = []
    for h in range(MEM_HEADS):
        sl = slice(h * MEM_HEAD_DIM, (h + 1) * MEM_HEAD_DIM)
        sc = _dot_nt(qm[:, sl].astype(BF16), km_ref[0, :, sl])
        p = jnp.exp(sc - jnp.max(sc, axis=-1, keepdims=True))
        p = p / jnp.sum(p, axis=-1, keepdims=True)
        heads.append(_dot(p.astype(BF16), vm_ref[0, :, sl]).astype(BF16))
    h2 = h1 + _dot(jnp.concatenate(heads, axis=1), wo_ref[...])
    h2_ref[0] = h2
    n3_ref[0] = _rms(h2, gffn_ref[...]).astype(BF16)


def _post_mixer(sbT, gb, cx, x, kmem, vmem, conv_w, g_sb, g_cv, w_out, g_x, w_q, w_o, g_ffn):
    b, s, d = x.shape
    nt = s // TOK_TILE
    m = kmem.shape[1]
    hb = TOK_TILE // SUBLANES
    const = lambda shape: pl.BlockSpec(shape, lambda bi, i: (0,) * len(shape))
    return pl.pallas_call(
        _post_kernel,
        grid=(b, nt),
        in_specs=[
            pl.BlockSpec((1, SB_WIDTH, TOK_TILE), lambda bi, i: (bi, 0, i)),
            pl.BlockSpec((1, TOK_TILE, CONV_CH), lambda bi, i: (bi, i, 0)),
            pl.BlockSpec((1, TOK_TILE, CONV_CH), lambda bi, i: (bi, i, 0)),
            pl.BlockSpec((1, SUBLANES, CONV_CH), lambda bi, i: (bi, jnp.maximum(i * hb - 1, 0), 0)),
            pl.BlockSpec((1, TOK_TILE, d), lambda bi, i: (bi, i, 0)),
            pl.BlockSpec((1, m, d), lambda bi, i: (bi, 0, 0)),
            pl.BlockSpec((1, m, d), lambda bi, i: (bi, 0, 0)),
            const((CONV_K, CONV_CH)),
            const((1, SB_WIDTH)),
            const((1, CONV_CH)),
            const((d, d)),
            const((1, d)),
            const((d, d)),
            const((d, d)),
            const((1, d)),
        ],
        out_specs=[
            pl.BlockSpec((1, TOK_TILE, d), lambda bi, i: (bi, i, 0)),
            pl.BlockSpec((1, TOK_TILE, d), lambda bi, i: (bi, i, 0)),
        ],
        out_shape=[
            jax.ShapeDtypeStruct((b, s, d), F32),
            jax.ShapeDtypeStruct((b, s, d), BF16),
        ],
        compiler_params=pltpu.CompilerParams(vmem_limit_bytes=VMEM_LIMIT),
        name="post_mixer",
    )(sbT, gb, cx, cx, x, kmem, vmem, conv_w, g_sb.reshape(1, -1), g_cv.reshape(1, -1), w_out,
      g_x.reshape(1, d), w_q, w_o, g_ffn.reshape(1, d))


def _top16(s):
    nk, nt = s.shape
    key_id = lax.broadcasted_iota(jnp.int32, (nk, nt), 0).astype(F32)
    slot = lax.broadcasted_iota(jnp.int32, (PEER_TOPK, nt), 0)
    cur = s
    rank = jnp.full((nk, nt), float(PEER_TOPK), F32)
    top = jnp.zeros((PEER_TOPK, nt), F32)
    for r in range(PEER_TOPK):
        m = jnp.max(cur, axis=0, keepdims=True)
        first = jnp.min(jnp.where(cur == m, key_id, float(nk)), axis=0, keepdims=True)
        sel = key_id == first
        rank = jnp.where(sel, float(r), rank)
        cur = jnp.where(sel, -jnp.inf, cur)
        top = jnp.where(slot == r, m, top)
    return rank, top


def _top16_distinct(s, want_rank):
    nk, nt = s.shape
    slot = lax.broadcasted_iota(jnp.int32, (PEER_TOPK, nt), 0)
    cur = s
    rank = jnp.full((nk, nt), float(PEER_TOPK), F32)
    top = jnp.zeros((PEER_TOPK, nt), F32)
    for r in range(PEER_TOPK):
        m = jnp.max(cur, axis=0, keepdims=True)
        hit = cur == m
        if want_rank:
            rank = jnp.where(hit, float(r), rank)
        cur = jnp.where(hit, -jnp.inf, cur)
        top = jnp.where(slot == r, m, top)
    taken = jnp.sum(jnp.where(cur == -jnp.inf, 1.0, 0.0), axis=0, keepdims=True)
    return rank, top, jnp.abs(taken - float(PEER_TOPK))


def _staircase(top1, top2):
    nt = top1.shape[1]
    row = lax.broadcasted_iota(jnp.int32, (PEER_TOPK, nt), 0).astype(F32)
    cnt = jnp.zeros((PEER_TOPK, nt), F32)
    front = top1 + top2[0:1]
    for _ in range(PEER_TOPK):
        m = jnp.max(front, axis=0, keepdims=True)
        a = jnp.min(jnp.where(front == m, row, float(PEER_TOPK)), axis=0, keepdims=True)
        sel = row == a
        cnt = jnp.where(sel, cnt + 1.0, cnt)
        nxt = jnp.sum(jnp.where(sel, cnt, 0.0), axis=0, keepdims=True)
        val = jnp.sum(jnp.where(row == nxt, top2, 0.0), axis=0, keepdims=True)
        val = jnp.where(nxt >= float(PEER_TOPK), -jnp.inf, val)
        front = jnp.where(sel, top1 + val, front)
    return cnt


def _gelu(x):
    return 0.5 * x * (1.0 + lax.erf(x * (2.0 ** -0.5)))


def _peer_kernel(n3_ref, h2_ref, wqT_ref, sk_ref, u_ref, vT_ref, gf_ref, out_ref,
                 acc_ref, st_ref, coef_ref, qT_ref, s_ref, rb_ref, eb_ref, ci_ref, ea_ref):
    c = pl.program_id(1)
    nchunks = pl.num_programs(1)

    @pl.when(c == 0)
    def _prepare():
        acc_ref[...] = jnp.zeros_like(acc_ref)
        qT_ref[...] = _dot_nt(wqT_ref[...], n3_ref[...]).astype(BF16)
        for hc in range(2 * PEER_HEADS):
            s = _dot(sk_ref[hc], qT_ref[hc * PEER_HALF:(hc + 1) * PEER_HALF, :])
            for lc in range(PEER_LC):
                s_ref[hc, lc] = s[:, lc * LANES:(lc + 1) * LANES]

        def select(h, lc, exact):
            s1 = s_ref[2 * h, lc]
            s2 = s_ref[2 * h + 1, lc]
            if exact:
                rank1, top1 = _top16(s1)
                rank2, top2 = _top16(s2)
                tie = None
            else:
                _, top1, tie1 = _top16_distinct(s1, False)
                rank2, top2, tie2 = _top16_distinct(s2, True)
                tie = jnp.maximum(tie1, tie2)
            cnt = _staircase(top1, top2)
            m1 = top1[0:1]
            m2 = top2[0:1]
            e1 = jnp.exp(top1 - m1)
            e2 = jnp.exp(top2 - m2)
            pref = jnp.zeros_like(cnt)
            for b in range(PEER_TOPK):
                pref = pref + jnp.where(cnt > float(b), e2[b:b + 1], 0.0)
            z = jnp.sum(e1 * pref, axis=0, keepdims=True)
            ci = jnp.zeros_like(s1)
            for a in range(PEER_TOPK):
                in_row = (rank1 == float(a)) if exact else (s1 == top1[a:a + 1])
                ci = jnp.where(in_row, cnt[a:a + 1], ci)
            rb_ref[h, lc] = rank2.astype(BF16)
            eb_ref[h, lc] = jnp.exp(s2 - m2).astype(BF16)
            ci_ref[h, lc] = ci
            ea_ref[h, lc] = jnp.exp(s1 - m1) / z
            return tie

        def per_head(h, carry):
            tie = select(h, 0, False)
            for lc in range(1, PEER_LC):
                tie = jnp.maximum(tie, select(h, lc, False))

            @pl.when(jnp.max(tie) > 0.0)
            def _with_ties():
                def slow(lc, inner):
                    select(h, lc, True)
                    return inner

                lax.fori_loop(0, PEER_LC, slow, 0)

            return carry

        lax.fori_loop(0, PEER_HEADS, per_head, 0)

    nsub = PEER_ECHUNK // PEER_SUB
    one = jnp.where(c >= 0, 1.0, 0.0).astype(BF16)

    def pre_activations(sub):
        rows = slice(sub * PEER_SUB, (sub + 1) * PEER_SUB)
        st_ref[rows, :] = _dot_nt(u_ref[rows, :], n3_ref[...])

    pre_activations(0)
    for sub in range(nsub):
        rows = slice(sub * PEER_SUB, (sub + 1) * PEER_SUB)
        if sub + 1 < nsub:
            pre_activations(sub + 1)
        ils = range(sub * PEER_SUB // PEER_KEYS, (sub + 1) * PEER_SUB // PEER_KEYS)
        for lc in range(PEER_LC):
            ls = slice(lc * LANES, (lc + 1) * LANES)
            for k0 in range(0, PEER_KEYS, PEER_KSLAB):
                ks = slice(k0, k0 + PEER_KSLAB)
                gates = [jnp.zeros((PEER_KSLAB, LANES), BF16) for _ in ils]
                for h in range(PEER_HEADS):
                    rb = rb_ref[h, lc, ks, :] * one
                    eb = eb_ref[h, lc, ks, :] * one
                    for r, il in enumerate(ils):
                        i = c * PEER_ROWS + il
                        keep = rb < ci_ref[h, lc, pl.ds(i, 1), :].astype(BF16)
                        w = eb * ea_ref[h, lc, pl.ds(i, 1), :].astype(BF16)
                        gates[r] = gates[r] + jnp.where(keep, w, jnp.zeros_like(w))
                for r, il in enumerate(ils):
                    rs = slice(il * PEER_KEYS + k0, il * PEER_KEYS + k0 + PEER_KSLAB)
                    coef_ref[rs, ls] = gates[r] * _gelu(st_ref[rs, ls]).astype(BF16)
        acc_ref[...] += _dot(vT_ref[:, rows], coef_ref[rows, :])

    @pl.when(c == nchunks - 1)
    def _finish():
        out_ref[...] = _rms(h2_ref[...] + acc_ref[...].T, gf_ref[...])


def _peer(n3, h2, wqT, sk, u, vT, g_final):
    t, d = n3.shape
    nt = t // PEER_TOK
    nc = PEER_EXPERTS // PEER_ECHUNK
    tile4 = (PEER_HEADS, PEER_LC, PEER_KEYS, LANES)
    return pl.pallas_call(
        _peer_kernel,
        grid=(nt, nc),
        in_specs=[
            pl.BlockSpec((PEER_TOK, d), lambda i, c: (i, 0)),
            pl.BlockSpec((PEER_TOK, d), lambda i, c: (i, 0)),
            pl.BlockSpec((PEER_HEADS * PEER_QDIM, d), lambda i, c: (0, 0)),
            pl.BlockSpec((2 * PEER_HEADS, PEER_KEYS, PEER_HALF), lambda i, c: (0, 0, 0)),
            pl.BlockSpec((PEER_ECHUNK, d), lambda i, c: (c, 0)),
            pl.BlockSpec((d, PEER_ECHUNK), lambda i, c: (0, c)),
            pl.BlockSpec((1, d), lambda i, c: (0, 0)),
        ],
        out_specs=pl.BlockSpec((PEER_TOK, d), lambda i, c: (i, 0)),
        out_shape=jax.ShapeDtypeStruct((t, d), F32),
        scratch_shapes=[
            pltpu.VMEM((d, PEER_TOK), F32),
            pltpu.VMEM((PEER_ECHUNK, PEER_TOK), F32),
            pltpu.VMEM((PEER_ECHUNK, PEER_TOK), BF16),
            pltpu.VMEM((PEER_HEADS * PEER_QDIM, PEER_TOK), BF16),
            pltpu.VMEM((2 * PEER_HEADS, PEER_LC, PEER_KEYS, LANES), F32),
            pltpu.VMEM(tile4, BF16),
            pltpu.VMEM(tile4, BF16),
            pltpu.VMEM(tile4, F32),
            pltpu.VMEM(tile4, F32),
        ],
        compiler_params=pltpu.CompilerParams(vmem_limit_bytes=VMEM_LIMIT),
        name="peer",
    )(n3, h2, wqT, sk, u, vT, g_final.reshape(1, d))


def kernel(x, mem, g_mix, w_in, conv_w, g_sb_out, g_conv_out, w_out, g_xattn, g_mem, w_q_mem, w_kv_mem, w_o_mem, g_ffn, w_query, sub_keys, expert_u, expert_v, g_final):
    b, s, d = x.shape
    assert g_mix.shape[0] == 1, "the final rmsnorm is fused into the single layer's PEER kernel"
    l = 0
    qT, k, vT, gb, cx = _in_proj(x, g_mix[l], w_in[l].astype(BF16))
    sbT = _attention(qT, k, vT)
    kmem, vmem = _mem_kv(mem, g_mem[l], w_kv_mem[l].astype(BF16))
    h2, n3 = _post_mixer(sbT, gb, cx, x, kmem, vmem, conv_w[l], g_sb_out[l], g_conv_out[l],
                         w_out[l].astype(BF16), g_xattn[l], w_q_mem[l].astype(BF16),
                         w_o_mem[l].astype(BF16), g_ffn[l])
    out = _peer(n3.reshape(b * s, d), h2.reshape(b * s, d),
                w_query[l].T.astype(BF16),
                sub_keys[l].reshape(2 * PEER_HEADS, PEER_KEYS, PEER_HALF).astype(BF16),
                expert_u[l].astype(BF16), expert_v[l].T.astype(BF16), g_final)
    return out.reshape(b, s, d)
```

```python
import functools
import math

import jax
import jax.numpy as jnp
from jax import lax
from jax.experimental import pallas as pl
from jax.experimental.pallas import tpu as pltpu

F32 = jnp.float32
BF16 = jnp.bfloat16

D_MODEL = 1024
EPS = 1e-6
SB_HEADS = 8
SB_HEAD_DIM = 64
SB_WIDTH = SB_HEADS * SB_HEAD_DIM
CONV_CH = D_MODEL - SB_WIDTH
CONV_K = 3
IN_COLS = 3 * SB_WIDTH + 3 * CONV_CH
MEM_HEADS = 4
MEM_HEAD_DIM = D_MODEL // MEM_HEADS
PEER_HEADS = 8
PEER_KEYS = 128
PEER_EXPERTS = PEER_KEYS * PEER_KEYS
PEER_QDIM = 256
PEER_HALF = PEER_QDIM // 2
PEER_TOPK = 16

LANES = 128
SUBLANES = 8

TOK_TILE = 512
ATT_BLOCK = 256
HEADS_PER_STEP = LANES // SB_HEAD_DIM
PEER_TOK = 512
PEER_LC = PEER_TOK // LANES
PEER_ECHUNK = 1024
PEER_ROWS = PEER_ECHUNK // PEER_KEYS
PEER_KSLAB = 64
VMEM_LIMIT = 58 * 1024 * 1024
LOG2E = math.log2(math.e)
EXP2_CLAMP = 126.0


def _rms(x, g):
    return x * lax.rsqrt(jnp.mean(x * x, axis=-1, keepdims=True) + EPS) * g


def _dot(a, b):
    return jnp.dot(a, b, preferred_element_type=F32)


def _dot_nt(a, b):
    return lax.dot_general(a, b, (((1,), (1,)), ((), ())), preferred_element_type=F32)


def _inproj_kernel(x_ref, g_ref, w_ref, qT_ref, k_ref, vT_ref, gb_ref, cx_ref):
    n = _rms(x_ref[0], g_ref[...]).astype(BF16)
    proj = _dot(n, w_ref[...])
    q = proj[:, 0:SB_WIDTH] * (SB_HEAD_DIM ** -0.5 * LOG2E)
    qT_ref[0] = q.T.astype(BF16)
    k_ref[0] = proj[:, SB_WIDTH:2 * SB_WIDTH].astype(BF16)
    vT = proj[:, 2 * SB_WIDTH:3 * SB_WIDTH].T.astype(BF16)
    for j in range(TOK_TILE // ATT_BLOCK):
        vT_ref[0, j] = vT[:, j * ATT_BLOCK:(j + 1) * ATT_BLOCK]
    o = 3 * SB_WIDTH
    gb_ref[0] = proj[:, o:o + CONV_CH]
    cx_ref[0] = proj[:, o + CONV_CH:o + 2 * CONV_CH] * proj[:, o + 2 * CONV_CH:o + 3 * CONV_CH]


def _in_proj(x, g_mix, w_in):
    b, s, d = x.shape
    nt = s // TOK_TILE
    return pl.pallas_call(
        _inproj_kernel,
        grid=(b, nt),
        in_specs=[
            pl.BlockSpec((1, TOK_TILE, d), lambda bi, i: (bi, i, 0)),
            pl.BlockSpec((1, d), lambda bi, i: (0, 0)),
            pl.BlockSpec((d, IN_COLS), lambda bi, i: (0, 0)),
        ],
        out_specs=[
            pl.BlockSpec((1, SB_WIDTH, TOK_TILE), lambda bi, i: (bi, 0, i)),
            pl.BlockSpec((1, TOK_TILE, SB_WIDTH), lambda bi, i: (bi, i, 0)),
            pl.BlockSpec((1, TOK_TILE // ATT_BLOCK, SB_WIDTH, ATT_BLOCK), lambda bi, i: (bi, i, 0, 0)),
            pl.BlockSpec((1, TOK_TILE, CONV_CH), lambda bi, i: (bi, i, 0)),
            pl.BlockSpec((1, TOK_TILE, CONV_CH), lambda bi, i: (bi, i, 0)),
        ],
        out_shape=[
            jax.ShapeDtypeStruct((b, SB_WIDTH, s), BF16),
            jax.ShapeDtypeStruct((b, s, SB_WIDTH), BF16),
            jax.ShapeDtypeStruct((b, s // ATT_BLOCK, SB_WIDTH, ATT_BLOCK), BF16),
            jax.ShapeDtypeStruct((b, s, CONV_CH), F32),
            jax.ShapeDtypeStruct((b, s, CONV_CH), F32),
        ],
        compiler_params=pltpu.CompilerParams(vmem_limit_bytes=VMEM_LIMIT),
        name="in_proj",
    )(x, g_mix.reshape(1, d), w_in)


def _attn_kernel(qT_ref, k_ref, vT_ref, u_ref, o_ref, qm_ref, acc_ref, c_ref, s_ref,
                 z_ref, n_ref, a_ref):
    qi = pl.program_id(1)
    tb = ATT_BLOCK
    feat = lax.broadcasted_iota(jnp.int32, (LANES, tb), 0)
    for h in range(SB_HEADS):
        p, hh = divmod(h, HEADS_PER_STEP)
        slab = qT_ref[0, p * LANES:(p + 1) * LANES, :]
        in_head = (feat >= hh * SB_HEAD_DIM) & (feat < (hh + 1) * SB_HEAD_DIM)
        qm_ref[h] = jnp.where(in_head, slab, jnp.zeros_like(slab))
    acc_ref[...] = jnp.zeros_like(acc_ref)
    c_ref[...] = jnp.zeros_like(c_ref)
    key_pos = lax.broadcasted_iota(jnp.int32, (tb, tb), 0)
    qry_pos = lax.broadcasted_iota(jnp.int32, (tb, tb), 1)
    past = key_pos < qry_pos

    def block(jb, diagonal):
        r0 = pl.multiple_of(jb * tb, tb)
        for p in range(SB_WIDTH // LANES):
            kb = k_ref[0, pl.ds(r0, tb), p * LANES:(p + 1) * LANES]
            for hh in range(HEADS_PER_STEP):
                h = p * HEADS_PER_STEP + hh
                z = _dot(kb, qm_ref[h])
                n = jnp.maximum(z, jnp.log(1.0 + jnp.exp2(jnp.minimum(z, EXP2_CLAMP))) * LOG2E)
                if diagonal:
                    n = jnp.where(past, n, 0.0)
                z_ref[h] = z
                n_ref[h] = n.astype(BF16)
        for h in range(SB_HEADS):
            cum = _dot(u_ref[...], n_ref[h])
            a = jnp.exp2(z_ref[h] - cum)
            if diagonal:
                a = jnp.where(past, a, 0.0)
            a_ref[h] = a.astype(BF16)
            s_ref[h:h + 1, :] = cum[0:1, :]
        for h in range(SB_HEADS):
            vb = vT_ref[0, jb, h * SB_HEAD_DIM:(h + 1) * SB_HEAD_DIM, :]
            c = c_ref[h:h + 1, :]
            acc_ref[h] += jnp.exp2(-c) * _dot(vb, a_ref[h])
            c_ref[h:h + 1, :] = c + s_ref[h:h + 1, :]

    block(qi, True)

    def body(j, carry):
        block(qi - 1 - j, False)
        return carry

    lax.fori_loop(0, qi, body, 0)
    for h in range(SB_HEADS):
        o_ref[0, h * SB_HEAD_DIM:(h + 1) * SB_HEAD_DIM, :] = acc_ref[h]


def _attention(qT, k, vT):
    b, _, s = qT.shape
    nq = s // ATT_BLOCK
    tri = (lax.broadcasted_iota(jnp.int32, (ATT_BLOCK, ATT_BLOCK), 1)
           >= lax.broadcasted_iota(jnp.int32, (ATT_BLOCK, ATT_BLOCK), 0)).astype(BF16)
    return pl.pallas_call(
        _attn_kernel,
        grid=(b, nq),
        in_specs=[
            pl.BlockSpec((1, SB_WIDTH, ATT_BLOCK), lambda bi, q: (bi, 0, q)),
            pl.BlockSpec((1, s, SB_WIDTH), lambda bi, q: (bi, 0, 0)),
            pl.BlockSpec((1, nq, SB_WIDTH, ATT_BLOCK), lambda bi, q: (bi, 0, 0, 0)),
            pl.BlockSpec((ATT_BLOCK, ATT_BLOCK), lambda bi, q: (0, 0)),
        ],
        out_specs=pl.BlockSpec((1, SB_WIDTH, ATT_BLOCK), lambda bi, q: (bi, 0, q)),
        out_shape=jax.ShapeDtypeStruct((b, SB_WIDTH, s), F32),
        scratch_shapes=[
            pltpu.VMEM((SB_HEADS, LANES, ATT_BLOCK), BF16),
            pltpu.VMEM((SB_HEADS, SB_HEAD_DIM, ATT_BLOCK), F32),
            pltpu.VMEM((SB_HEADS, ATT_BLOCK), F32),
            pltpu.VMEM((SB_HEADS, ATT_BLOCK), F32),
            pltpu.VMEM((SB_HEADS, ATT_BLOCK, ATT_BLOCK), F32),
            pltpu.VMEM((SB_HEADS, ATT_BLOCK, ATT_BLOCK), BF16),
            pltpu.VMEM((SB_HEADS, ATT_BLOCK, ATT_BLOCK), BF16),
        ],
        compiler_params=pltpu.CompilerParams(vmem_limit_bytes=VMEM_LIMIT),
        name="sb_attention",
    )(qT, k, vT, tri)


def _memkv_kernel(mem_ref, g_ref, w_ref, k_ref, v_ref):
    mn = _rms(mem_ref[0], g_ref[...]).astype(BF16)
    kv = _dot(mn, w_ref[...])
    k_ref[0] = kv[:, :D_MODEL].astype(BF16)
    v_ref[0] = kv[:, D_MODEL:].astype(BF16)


def _mem_kv(mem, g_mem, w_kv):
    b, m, d = mem.shape
    return pl.pallas_call(
        _memkv_kernel,
        grid=(b,),
        in_specs=[
            pl.BlockSpec((1, m, d), lambda bi: (bi, 0, 0)),
            pl.BlockSpec((1, d), lambda bi: (0, 0)),
            pl.BlockSpec((d, 2 * d), lambda bi: (0, 0)),
        ],
        out_specs=[pl.BlockSpec((1, m, d), lambda bi: (bi, 0, 0))] * 2,
        out_shape=[jax.ShapeDtypeStruct((b, m, d), BF16)] * 2,
        compiler_params=pltpu.CompilerParams(vmem_limit_bytes=VMEM_LIMIT),
        name="mem_kv",
    )(mem, g_mem.reshape(1, d), w_kv)


def _post_kernel(sbT_ref, gb_ref, cx_ref, halo_ref, x_ref, km_ref, vm_ref, cw_ref,
                 gsb_ref, gcv_ref, wout_ref, gx_ref, wq_ref, wo_ref, gffn_ref,
                 h2_ref, n3_ref):
    i = pl.program_id(1)
    tm = TOK_TILE
    sbn = _rms(sbT_ref[0].T, gsb_ref[...])

    cx = cx_ref[0]
    halo = jnp.where(i > 0, halo_ref[0], 0.0)
    row = lax.broadcasted_iota(jnp.int32, (tm, CONV_CH), 0)
    cx1 = jnp.where(row == 0, halo[7:8], pltpu.roll(cx, 1, 0))
    cx2 = jnp.where(row == 0, halo[6:7],
                    jnp.where(row == 1, halo[7:8], pltpu.roll(cx, 2, 0)))
    cw = cw_ref[...]
    conv = gb_ref[0] * (cw[0:1] * cx2 + cw[1:2] * cx1 + cw[2:3] * cx)
    cvn = _rms(conv, gcv_ref[...])

    h1 = (x_ref[0] + _dot(sbn.astype(BF16), wout_ref[0:SB_WIDTH, :])
          + _dot(cvn.astype(BF16), wout_ref[SB_WIDTH:, :]))

    n2 = _rms(h1, gx_ref[...]).astype(BF16)
    qm = _dot(n2, wq_ref[...]) * (MEM_HEAD_DIM ** -0.5)
    heads = []
    for h in range(MEM_HEADS):
        sl = slice(h * MEM_HEAD_DIM, (h + 1) * MEM_HEAD_DIM)
        sc = _dot_nt(qm[:, sl].astype(BF16), km_ref[0, :, sl])
        p = jnp.exp(sc - jnp.max(sc, axis=-1, keepdims=True))
        p = p / jnp.sum(p, axis=-1, keepdims=True)
        heads.append(_dot(p.astype(BF16), vm_ref[0, :, sl]).astype(BF16))
    h2 = h1 + _dot(jnp.concatenate(heads, axis=1), wo_ref[...])
    h2_ref[0] = h2
    n3_ref[0] = _rms(h2, gffn_ref[...]).astype(BF16)


def _post_mixer(sbT, gb, cx, x, kmem, vmem, conv_w, g_sb, g_cv, w_out, g_x, w_q, w_o, g_ffn):
    b, s, d = x.shape
    nt = s // TOK_TILE
    m = kmem.shape[1]
    hb = TOK_TILE // SUBLANES
    const = lambda shape: pl.BlockSpec(shape, lambda bi, i: (0,) * len(shape))
    return pl.pallas_call(
        _post_kernel,
        grid=(b, nt),
        in_specs=[
            pl.BlockSpec((1, SB_WIDTH, TOK_TILE), lambda bi, i: (bi, 0, i)),
            pl.BlockSpec((1, TOK_TILE, CONV_CH), lambda bi, i: (bi, i, 0)),
            pl.BlockSpec((1, TOK_TILE, CONV_CH), lambda bi, i: (bi, i, 0)),
            pl.BlockSpec((1, SUBLANES, CONV_CH), lambda bi, i: (bi, jnp.maximum(i * hb - 1, 0), 0)),
            pl.BlockSpec((1, TOK_TILE, d), lambda bi, i: (bi, i, 0)),
            pl.BlockSpec((1, m, d), lambda bi, i: (bi, 0, 0)),
            pl.BlockSpec((1, m, d), lambda bi, i: (bi, 0, 0)),
            const((CONV_K, CONV_CH)),
            const((1, SB_WIDTH)),
            const((1, CONV_CH)),
            const((d, d)),
            const((1, d)),
            const((d, d)),
            const((d, d)),
            const((1, d)),
        ],
        out_specs=[
            pl.BlockSpec((1, TOK_TILE, d), lambda bi, i: (bi, i, 0)),
            pl.BlockSpec((1, TOK_TILE, d), lambda bi, i: (bi, i, 0)),
        ],
        out_shape=[
            jax.ShapeDtypeStruct((b, s, d), F32),
            jax.ShapeDtypeStruct((b, s, d), BF16),
        ],
        compiler_params=pltpu.CompilerParams(vmem_limit_bytes=VMEM_LIMIT),
        name="post_mixer",
    )(sbT, gb, cx, cx, x, kmem, vmem, conv_w, g_sb.reshape(1, -1), g_cv.reshape(1, -1), w_out,
      g_x.reshape(1, d), w_q, w_o, g_ffn.reshape(1, d))


def _top16(s):
    nk, nt = s.shape
    key_id = lax.broadcasted_iota(jnp.int32, (nk, nt), 0).astype(F32)
    slot = lax.broadcasted_iota(jnp.int32, (PEER_TOPK, nt), 0)
    cur = s
    rank = jnp.full((nk, nt), float(PEER_TOPK), F32)
    top = jnp.zeros((PEER_TOPK, nt), F32)
    for r in range(PEER_TOPK):
        m = jnp.max(cur, axis=0, keepdims=True)
        first = jnp.min(jnp.where(cur == m, key_id, float(nk)), axis=0, keepdims=True)
        sel = key_id == first
        rank = jnp.where(sel, float(r), rank)
        cur = jnp.where(sel, -jnp.inf, cur)
        top = jnp.where(slot == r, m, top)
    return rank, top


def _top16_distinct(s, want_rank):
    nk, nt = s.shape
    slot = lax.broadcasted_iota(jnp.int32, (PEER_TOPK, nt), 0)
    cur = s
    rank = jnp.full((nk, nt), float(PEER_TOPK), F32)
    top = jnp.zeros((PEER_TOPK, nt), F32)
    for r in range(PEER_TOPK):
        m = jnp.max(cur, axis=0, keepdims=True)
        hit = cur == m
        if want_rank:
            rank = jnp.where(hit, float(r), rank)
        cur = jnp.where(hit, -jnp.inf, cur)
        top = jnp.where(slot == r, m, top)
    taken = jnp.sum(jnp.where(cur == -jnp.inf, 1.0, 0.0), axis=0, keepdims=True)
    return rank, top, jnp.abs(taken - float(PEER_TOPK))


def _staircase(top1, top2):
    nt = top1.shape[1]
    row = lax.broadcasted_iota(jnp.int32, (PEER_TOPK, nt), 0).astype(F32)
    cnt = jnp.zeros((PEER_TOPK, nt), F32)
    front = top1 + top2[0:1]
    for _ in range(PEER_TOPK):
        m = jnp.max(front, axis=0, keepdims=True)
        a = jnp.min(jnp.where(front == m, row, float(PEER_TOPK)), axis=0, keepdims=True)
        sel = row == a
        cnt = jnp.where(sel, cnt + 1.0, cnt)
        nxt = jnp.sum(jnp.where(sel, cnt, 0.0), axis=0, keepdims=True)
        val = jnp.sum(jnp.where(row == nxt, top2, 0.0), axis=0, keepdims=True)
        val = jnp.where(nxt >= float(PEER_TOPK), -jnp.inf, val)
        front = jnp.where(sel, top1 + val, front)
    return cnt


def _gelu(x):
    return 0.5 * x * (1.0 + lax.erf(x * (2.0 ** -0.5)))


def _peer_kernel(n3_ref, h2_ref, wqT_ref, sk_ref, u_ref, vT_ref, gf_ref, out_ref,
                 acc_ref, act_ref, coef_ref, qT_ref, s_ref, rb_ref, eb_ref, ci_ref, ea_ref):
    c = pl.program_id(1)
    nchunks = pl.num_programs(1)

    @pl.when(c == 0)
    def _prepare():
        acc_ref[...] = jnp.zeros_like(acc_ref)
        qT_ref[...] = _dot_nt(wqT_ref[...], n3_ref[...]).astype(BF16)
        for hc in range(2 * PEER_HEADS):
            s = _dot(sk_ref[hc], qT_ref[hc * PEER_HALF:(hc + 1) * PEER_HALF, :])
            for lc in range(PEER_LC):
                s_ref[hc, lc] = s[:, lc * LANES:(lc + 1) * LANES]

        def select(h, lc, exact):
            s1 = s_ref[2 * h, lc]
            s2 = s_ref[2 * h + 1, lc]
            if exact:
                rank1, top1 = _top16(s1)
                rank2, top2 = _top16(s2)
                tie = None
            else:
                _, top1, tie1 = _top16_distinct(s1, False)
                rank2, top2, tie2 = _top16_distinct(s2, True)
                tie = jnp.maximum(tie1, tie2)
            cnt = _staircase(top1, top2)
            m1 = top1[0:1]
            m2 = top2[0:1]
            e1 = jnp.exp(top1 - m1)
            e2 = jnp.exp(top2 - m2)
            pref = jnp.zeros_like(cnt)
            for b in range(PEER_TOPK):
                pref = pref + jnp.where(cnt > float(b), e2[b:b + 1], 0.0)
            z = jnp.sum(e1 * pref, axis=0, keepdims=True)
            ci = jnp.zeros_like(s1)
            for a in range(PEER_TOPK):
                in_row = (rank1 == float(a)) if exact else (s1 == top1[a:a + 1])
                ci = jnp.where(in_row, cnt[a:a + 1], ci)
            rb_ref[h, lc] = rank2.astype(BF16)
            eb_ref[h, lc] = jnp.exp(s2 - m2).astype(BF16)
            ci_ref[h, lc] = ci
            ea_ref[h, lc] = jnp.exp(s1 - m1) / z
            return tie

        def per_head(h, carry):
            tie = select(h, 0, False)
            for lc in range(1, PEER_LC):
                tie = jnp.maximum(tie, select(h, lc, False))

            @pl.when(jnp.max(tie) > 0.0)
            def _with_ties():
                def slow(lc, inner):
                    select(h, lc, True)
                    return inner

                lax.fori_loop(0, PEER_LC, slow, 0)

            return carry

        lax.fori_loop(0, PEER_HEADS, per_head, 0)

    act_ref[...] = _gelu(_dot_nt(u_ref[...], n3_ref[...])).astype(BF16)
    one = jnp.where(c >= 0, 1.0, 0.0).astype(BF16)
    i0 = pl.multiple_of(c * PEER_ROWS, PEER_ROWS)
    for lc in range(PEER_LC):
        ls = slice(lc * LANES, (lc + 1) * LANES)
        for k0 in range(0, PEER_KEYS, PEER_KSLAB):
            ks = slice(k0, k0 + PEER_KSLAB)
            gates = [jnp.zeros((PEER_KSLAB, LANES), BF16) for _ in range(PEER_ROWS)]
            for h in range(PEER_HEADS):
                rb = rb_ref[h, lc, ks, :] * one
                eb = eb_ref[h, lc, ks, :] * one
                ci8 = ci_ref[h, lc, pl.ds(i0, PEER_ROWS), :]
                ea8 = ea_ref[h, lc, pl.ds(i0, PEER_ROWS), :]
                for il in range(PEER_ROWS):
                    keep = rb < ci8[il:il + 1, :].astype(BF16)
                    w = eb * ea8[il:il + 1, :].astype(BF16)
                    gates[il] = gates[il] + jnp.where(keep, w, jnp.zeros_like(w))
            for il in range(PEER_ROWS):
                rs = slice(il * PEER_KEYS + k0, il * PEER_KEYS + k0 + PEER_KSLAB)
                coef_ref[rs, ls] = gates[il] * act_ref[rs, ls]
    acc_ref[...] += _dot(vT_ref[...], coef_ref[...])

    @pl.when(c == nchunks - 1)
    def _finish():
        out_ref[...] = _rms(h2_ref[...] + acc_ref[...].T, gf_ref[...])


def _peer(n3, h2, wqT, sk, u, vT, g_final):
    t, d = n3.shape
    nt = t // PEER_TOK
    nc = PEER_EXPERTS // PEER_ECHUNK
    tile4 = (PEER_HEADS, PEER_LC, PEER_KEYS, LANES)
    return pl.pallas_call(
        _peer_kernel,
        grid=(nt, nc),
        in_specs=[
            pl.BlockSpec((PEER_TOK, d), lambda i, c: (i, 0)),
            pl.BlockSpec((PEER_TOK, d), lambda i, c: (i, 0)),
            pl.BlockSpec((PEER_HEADS * PEER_QDIM, d), lambda i, c: (0, 0)),
            pl.BlockSpec((2 * PEER_HEADS, PEER_KEYS, PEER_HALF), lambda i, c: (0, 0, 0)),
            pl.BlockSpec((PEER_ECHUNK, d), lambda i, c: (c, 0)),
            pl.BlockSpec((d, PEER_ECHUNK), lambda i, c: (0, c)),
            pl.BlockSpec((1, d), lambda i, c: (0, 0)),
        ],
        out_specs=pl.BlockSpec((PEER_TOK, d), lambda i, c: (i, 0)),
        out_shape=jax.ShapeDtypeStruct((t, d), F32),
        scratch_shapes=[
            pltpu.VMEM((d, PEER_TOK), F32),
            pltpu.VMEM((PEER_ECHUNK, PEER_TOK), BF16),
            pltpu.VMEM((PEER_ECHUNK, PEER_TOK), BF16),
            pltpu.VMEM((PEER_HEADS * PEER_QDIM, PEER_TOK), BF16),
            pltpu.VMEM((2 * PEER_HEADS, PEER_LC, PEER_KEYS, LANES), F32),
            pltpu.VMEM(tile4, BF16),
            pltpu.VMEM(tile4, BF16),
            pltpu.VMEM(tile4, F32),
            pltpu.VMEM(tile4, F32),
        ],
        compiler_params=pltpu.CompilerParams(vmem_limit_bytes=VMEM_LIMIT),
        name="peer",
    )(n3, h2, wqT, sk, u, vT, g_final.reshape(1, d))


def kernel(x, mem, g_mix, w_in, conv_w, g_sb_out, g_conv_out, w_out, g_xattn, g_mem, w_q_mem, w_kv_mem, w_o_mem, g_ffn, w_query, sub_keys, expert_u, expert_v, g_final):
    b, s, d = x.shape
    assert g_mix.shape[0] == 1, "the final rmsnorm is fused into the single layer's PEER kernel"
    l = 0
    qT, k, vT, gb, cx = _in_proj(x, g_mix[l], w_in[l].astype(BF16))
    sbT = _attention(qT, k, vT)
    kmem, vmem = _mem_kv(mem, g_mem[l], w_kv_mem[l].astype(BF16))
    h2, n3 = _post_mixer(sbT, gb, cx, x, kmem, vmem, conv_w[l], g_sb_out[l], g_conv_out[l],
                         w_out[l].astype(BF16), g_xattn[l], w_q_mem[l].astype(BF16),
                         w_o_mem[l].astype(BF16), g_ffn[l])
    out = _peer(n3.reshape(b * s, d), h2.reshape(b * s, d),
                w_query[l].T.astype(BF16),
                sub_keys[l].reshape(2 * PEER_HEADS, PEER_KEYS, PEER_HALF).astype(BF16),
                expert_u[l].astype(BF16), expert_v[l].T.astype(BF16), g_final)
    return out.reshape(b, s, d)
```

```python
import functools
import math

import jax
import jax.numpy as jnp
from jax import lax
from jax.experimental import pallas as pl
from jax.experimental.pallas import tpu as pltpu

F32 = jnp.float32
BF16 = jnp.bfloat16

D_MODEL = 1024
EPS = 1e-6
SB_HEADS = 8
SB_HEAD_DIM = 64
SB_WIDTH = SB_HEADS * SB_HEAD_DIM
CONV_CH = D_MODEL - SB_WIDTH
CONV_K = 3
IN_COLS = 3 * SB_WIDTH + 3 * CONV_CH
MEM_HEADS = 4
MEM_HEAD_DIM = D_MODEL // MEM_HEADS
PEER_HEADS = 8
PEER_KEYS = 128
PEER_EXPERTS = PEER_KEYS * PEER_KEYS
PEER_QDIM = 256
PEER_HALF = PEER_QDIM // 2
PEER_TOPK = 16

LANES = 128
SUBLANES = 8

TOK_TILE = 512
ATT_BLOCK = 256
ATT_UNROLL = 4
HEADS_PER_STEP = LANES // SB_HEAD_DIM
PEER_TOK = 512
PEER_LC = PEER_TOK // LANES
PEER_ECHUNK = 1024
PEER_ROWS = PEER_ECHUNK // PEER_KEYS
PEER_KSLAB = 64
VMEM_LIMIT = 58 * 1024 * 1024
LOG2E = math.log2(math.e)
EXP2_CLAMP = 126.0


def _rms(x, g):
    return x * lax.rsqrt(jnp.mean(x * x, axis=-1, keepdims=True) + EPS) * g


def _dot(a, b):
    return jnp.dot(a, b, preferred_element_type=F32)


def _dot_nt(a, b):
    return lax.dot_general(a, b, (((1,), (1,)), ((), ())), preferred_element_type=F32)


def _inproj_kernel(x_ref, g_ref, w_ref, qT_ref, k_ref, vT_ref, gb_ref, cx_ref):
    n = _rms(x_ref[0], g_ref[...]).astype(BF16)
    proj = _dot(n, w_ref[...])
    q = proj[:, 0:SB_WIDTH] * (SB_HEAD_DIM ** -0.5 * LOG2E)
    qT_ref[0] = q.T.astype(BF16)
    k_ref[0] = proj[:, SB_WIDTH:2 * SB_WIDTH].astype(BF16)
    vT = proj[:, 2 * SB_WIDTH:3 * SB_WIDTH].T.astype(BF16)
    for j in range(TOK_TILE // ATT_BLOCK):
        vT_ref[0, j] = vT[:, j * ATT_BLOCK:(j + 1) * ATT_BLOCK]
    o = 3 * SB_WIDTH
    gb_ref[0] = proj[:, o:o + CONV_CH]
    cx_ref[0] = proj[:, o + CONV_CH:o + 2 * CONV_CH] * proj[:, o + 2 * CONV_CH:o + 3 * CONV_CH]


def _in_proj(x, g_mix, w_in):
    b, s, d = x.shape
    nt = s // TOK_TILE
    return pl.pallas_call(
        _inproj_kernel,
        grid=(b, nt),
        in_specs=[
            pl.BlockSpec((1, TOK_TILE, d), lambda bi, i: (bi, i, 0)),
            pl.BlockSpec((1, d), lambda bi, i: (0, 0)),
            pl.BlockSpec((d, IN_COLS), lambda bi, i: (0, 0)),
        ],
        out_specs=[
            pl.BlockSpec((1, SB_WIDTH, TOK_TILE), lambda bi, i: (bi, 0, i)),
            pl.BlockSpec((1, TOK_TILE, SB_WIDTH), lambda bi, i: (bi, i, 0)),
            pl.BlockSpec((1, TOK_TILE // ATT_BLOCK, SB_WIDTH, ATT_BLOCK), lambda bi, i: (bi, i, 0, 0)),
            pl.BlockSpec((1, TOK_TILE, CONV_CH), lambda bi, i: (bi, i, 0)),
            pl.BlockSpec((1, TOK_TILE, CONV_CH), lambda bi, i: (bi, i, 0)),
        ],
        out_shape=[
            jax.ShapeDtypeStruct((b, SB_WIDTH, s), BF16),
            jax.ShapeDtypeStruct((b, s, SB_WIDTH), BF16),
            jax.ShapeDtypeStruct((b, s // ATT_BLOCK, SB_WIDTH, ATT_BLOCK), BF16),
            jax.ShapeDtypeStruct((b, s, CONV_CH), F32),
            jax.ShapeDtypeStruct((b, s, CONV_CH), F32),
        ],
        compiler_params=pltpu.CompilerParams(vmem_limit_bytes=VMEM_LIMIT),
        name="in_proj",
    )(x, g_mix.reshape(1, d), w_in)


def _attn_kernel(qT_ref, k_ref, vT_ref, u_ref, o_ref, qm_ref, acc_ref, c_ref, s_ref,
                 z_ref, n_ref, a_ref):
    qi = pl.program_id(1)
    tb = ATT_BLOCK
    feat = lax.broadcasted_iota(jnp.int32, (LANES, tb), 0)
    for h in range(SB_HEADS):
        p, hh = divmod(h, HEADS_PER_STEP)
        slab = qT_ref[0, p * LANES:(p + 1) * LANES, :]
        in_head = (feat >= hh * SB_HEAD_DIM) & (feat < (hh + 1) * SB_HEAD_DIM)
        qm_ref[h] = jnp.where(in_head, slab, jnp.zeros_like(slab))
    acc_ref[...] = jnp.zeros_like(acc_ref)
    c_ref[...] = jnp.zeros_like(c_ref)
    key_pos = lax.broadcasted_iota(jnp.int32, (tb, tb), 0)
    qry_pos = lax.broadcasted_iota(jnp.int32, (tb, tb), 1)
    past = key_pos < qry_pos

    def blocks(jbs, diagonal):
        for bank, jb in enumerate(jbs):
            r0 = pl.multiple_of(jb * tb, tb)
            for p in range(SB_WIDTH // LANES):
                kb = k_ref[0, pl.ds(r0, tb), p * LANES:(p + 1) * LANES]
                for hh in range(HEADS_PER_STEP):
                    h = p * HEADS_PER_STEP + hh
                    z = _dot(kb, qm_ref[h])
                    n = jnp.maximum(z, jnp.log(1.0 + jnp.exp2(jnp.minimum(z, EXP2_CLAMP))) * LOG2E)
                    if diagonal:
                        n = jnp.where(past, n, 0.0)
                    z_ref[bank, h] = z
                    n_ref[bank, h] = n.astype(BF16)
        for bank, jb in enumerate(jbs):
            for h in range(SB_HEADS):
                cum = _dot(u_ref[...], n_ref[bank, h])
                a = jnp.exp2(z_ref[bank, h] - cum)
                if diagonal:
                    a = jnp.where(past, a, 0.0)
                a_ref[bank, h] = a.astype(BF16)
                s_ref[bank, h:h + 1, :] = cum[0:1, :]
        for bank, jb in enumerate(jbs):
            for h in range(SB_HEADS):
                vb = vT_ref[0, jb, h * SB_HEAD_DIM:(h + 1) * SB_HEAD_DIM, :]
                c = c_ref[h:h + 1, :]
                acc_ref[h] += jnp.exp2(-c) * _dot(vb, a_ref[bank, h])
                c_ref[h:h + 1, :] = c + s_ref[bank, h:h + 1, :]

    blocks([qi], True)

    def body(j, carry):
        top = qi - 1 - ATT_UNROLL * j
        blocks([top - u for u in range(ATT_UNROLL)], False)
        return carry

    lax.fori_loop(0, qi // ATT_UNROLL, body, 0)
    rem = qi % ATT_UNROLL
    size = ATT_UNROLL // 2
    while size >= 1:
        @pl.when((rem & size) != 0)
        def _tail(size=size):
            top = (rem & (2 * size - 1)) - 1
            blocks([top - u for u in range(size)], False)
        size //= 2

    for h in range(SB_HEADS):
        o_ref[0, h * SB_HEAD_DIM:(h + 1) * SB_HEAD_DIM, :] = acc_ref[h]


def _attention(qT, k, vT):
    b, _, s = qT.shape
    nq = s // ATT_BLOCK
    tri = (lax.broadcasted_iota(jnp.int32, (ATT_BLOCK, ATT_BLOCK), 1)
           >= lax.broadcasted_iota(jnp.int32, (ATT_BLOCK, ATT_BLOCK), 0)).astype(BF16)
    return pl.pallas_call(
        _attn_kernel,
        grid=(b, nq),
        in_specs=[
            pl.BlockSpec((1, SB_WIDTH, ATT_BLOCK), lambda bi, q: (bi, 0, q)),
            pl.BlockSpec((1, s, SB_WIDTH), lambda bi, q: (bi, 0, 0)),
            pl.BlockSpec((1, nq, SB_WIDTH, ATT_BLOCK), lambda bi, q: (bi, 0, 0, 0)),
            pl.BlockSpec((ATT_BLOCK, ATT_BLOCK), lambda bi, q: (0, 0)),
        ],
        out_specs=pl.BlockSpec((1, SB_WIDTH, ATT_BLOCK), lambda bi, q: (bi, 0, q)),
        out_shape=jax.ShapeDtypeStruct((b, SB_WIDTH, s), F32),
        scratch_shapes=[
            pltpu.VMEM((SB_HEADS, LANES, ATT_BLOCK), BF16),
            pltpu.VMEM((SB_HEADS, SB_HEAD_DIM, ATT_BLOCK), F32),
            pltpu.VMEM((SB_HEADS, ATT_BLOCK), F32),
            pltpu.VMEM((ATT_UNROLL, SB_HEADS, ATT_BLOCK), F32),
            pltpu.VMEM((ATT_UNROLL, SB_HEADS, ATT_BLOCK, ATT_BLOCK), F32),
            pltpu.VMEM((ATT_UNROLL, SB_HEADS, ATT_BLOCK, ATT_BLOCK), BF16),
            pltpu.VMEM((ATT_UNROLL, SB_HEADS, ATT_BLOCK, ATT_BLOCK), BF16),
        ],
        compiler_params=pltpu.CompilerParams(vmem_limit_bytes=VMEM_LIMIT),
        name="sb_attention",
    )(qT, k, vT, tri)


def _memkv_kernel(mem_ref, g_ref, w_ref, k_ref, v_ref):
    mn = _rms(mem_ref[0], g_ref[...]).astype(BF16)
    kv = _dot(mn, w_ref[...])
    k_ref[0] = kv[:, :D_MODEL].astype(BF16)
    v_ref[0] = kv[:, D_MODEL:].astype(BF16)


def _mem_kv(mem, g_mem, w_kv):
    b, m, d = mem.shape
    return pl.pallas_call(
        _memkv_kernel,
        grid=(b,),
        in_specs=[
            pl.BlockSpec((1, m, d), lambda bi: (bi, 0, 0)),
            pl.BlockSpec((1, d), lambda bi: (0, 0)),
            pl.BlockSpec((d, 2 * d), lambda bi: (0, 0)),
        ],
        out_specs=[pl.BlockSpec((1, m, d), lambda bi: (bi, 0, 0))] * 2,
        out_shape=[jax.ShapeDtypeStruct((b, m, d), BF16)] * 2,
        compiler_params=pltpu.CompilerParams(vmem_limit_bytes=VMEM_LIMIT),
        name="mem_kv",
    )(mem, g_mem.reshape(1, d), w_kv)


def _post_kernel(sbT_ref, gb_ref, cx_ref, halo_ref, x_ref, km_ref, vm_ref, cw_ref,
                 gsb_ref, gcv_ref, wout_ref, gx_ref, wq_ref, wo_ref, gffn_ref,
                 h2_ref, n3_ref):
    i = pl.program_id(1)
    tm = TOK_TILE
    sbn = _rms(sbT_ref[0].T, gsb_ref[...])

    cx = cx_ref[0]
    halo = jnp.where(i > 0, halo_ref[0], 0.0)
    row = lax.broadcasted_iota(jnp.int32, (tm, CONV_CH), 0)
    cx1 = jnp.where(row == 0, halo[7:8], pltpu.roll(cx, 1, 0))
    cx2 = jnp.where(row == 0, halo[6:7],
                    jnp.where(row == 1, halo[7:8], pltpu.roll(cx, 2, 0)))
    cw = cw_ref[...]
    conv = gb_ref[0] * (cw[0:1] * cx2 + cw[1:2] * cx1 + cw[2:3] * cx)
    cvn = _rms(conv, gcv_ref[...])

    h1 = (x_ref[0] + _dot(sbn.astype(BF16), wout_ref[0:SB_WIDTH, :])
          + _dot(cvn.astype(BF16), wout_ref[SB_WIDTH:, :]))

    n2 = _rms(h1, gx_ref[...]).astype(BF16)
    qm = _dot(n2, wq_ref[...]) * (MEM_HEAD_DIM ** -0.5)
    heads = []
    for h in range(MEM_HEADS):
        sl = slice(h * MEM_HEAD_DIM, (h + 1) * MEM_HEAD_DIM)
        sc = _dot_nt(qm[:, sl].astype(BF16), km_ref[0, :, sl])
        p = jnp.exp(sc - jnp.max(sc, axis=-1, keepdims=True))
        p = p / jnp.sum(p, axis=-1, keepdims=True)
        heads.append(_dot(p.astype(BF16), vm_ref[0, :, sl]).astype(BF16))
    h2 = h1 + _dot(jnp.concatenate(heads, axis=1), wo_ref[...])
    h2_ref[0] = h2
    n3_ref[0] = _rms(h2, gffn_ref[...]).astype(BF16)


def _post_mixer(sbT, gb, cx, x, kmem, vmem, conv_w, g_sb, g_cv, w_out, g_x, w_q, w_o, g_ffn):
    b, s, d = x.shape
    nt = s // TOK_TILE
    m = kmem.shape[1]
    hb = TOK_TILE // SUBLANES
    const = lambda shape: pl.BlockSpec(shape, lambda bi, i: (0,) * len(shape))
    return pl.pallas_call(
        _post_kernel,
        grid=(b, nt),
        in_specs=[
            pl.BlockSpec((1, SB_WIDTH, TOK_TILE), lambda bi, i: (bi, 0, i)),
            pl.BlockSpec((1, TOK_TILE, CONV_CH), lambda bi, i: (bi, i, 0)),
            pl.BlockSpec((1, TOK_TILE, CONV_CH), lambda bi, i: (bi, i, 0)),
            pl.BlockSpec((1, SUBLANES, CONV_CH), lambda bi, i: (bi, jnp.maximum(i * hb - 1, 0), 0)),
            pl.BlockSpec((1, TOK_TILE, d), lambda bi, i: (bi, i, 0)),
            pl.BlockSpec((1, m, d), lambda bi, i: (bi, 0, 0)),
            pl.BlockSpec((1, m, d), lambda bi, i: (bi, 0, 0)),
            const((CONV_K, CONV_CH)),
            const((1, SB_WIDTH)),
            const((1, CONV_CH)),
            const((d, d)),
            const((1, d)),
            const((d, d)),
            const((d, d)),
            const((1, d)),
        ],
        out_specs=[
            pl.BlockSpec((1, TOK_TILE, d), lambda bi, i: (bi, i, 0)),
            pl.BlockSpec((1, TOK_TILE, d), lambda bi, i: (bi, i, 0)),
        ],
        out_shape=[
            jax.ShapeDtypeStruct((b, s, d), F32),
            jax.ShapeDtypeStruct((b, s, d), BF16),
        ],
        compiler_params=pltpu.CompilerParams(vmem_limit_bytes=VMEM_LIMIT),
        name="post_mixer",
    )(sbT, gb, cx, cx, x, kmem, vmem, conv_w, g_sb.reshape(1, -1), g_cv.reshape(1, -1), w_out,
      g_x.reshape(1, d), w_q, w_o, g_ffn.reshape(1, d))


def _top16(s):
    nk, nt = s.shape
    key_id = lax.broadcasted_iota(jnp.int32, (nk, nt), 0).astype(F32)
    slot = lax.broadcasted_iota(jnp.int32, (PEER_TOPK, nt), 0)
    cur = s
    rank = jnp.full((nk, nt), float(PEER_TOPK), F32)
    top = jnp.zeros((PEER_TOPK, nt), F32)
    for r in range(PEER_TOPK):
        m = jnp.max(cur, axis=0, keepdims=True)
        first = jnp.min(jnp.where(cur == m, key_id, float(nk)), axis=0, keepdims=True)
        sel = key_id == first
        rank = jnp.where(sel, float(r), rank)
        cur = jnp.where(sel, -jnp.inf, cur)
        top = jnp.where(slot == r, m, top)
    return rank, top


def _top16_distinct(s, want_rank):
    nk, nt = s.shape
    slot = lax.broadcasted_iota(jnp.int32, (PEER_TOPK, nt), 0)
    cur = s
    rank = jnp.full((nk, nt), float(PEER_TOPK), F32)
    top = jnp.zeros((PEER_TOPK, nt), F32)
    for r in range(PEER_TOPK):
        m = jnp.max(cur, axis=0, keepdims=True)
        hit = cur == m
        if want_rank:
            rank = jnp.where(hit, float(r), rank)
        cur = jnp.where(hit, -jnp.inf, cur)
        top = jnp.where(slot == r, m, top)
    taken = jnp.sum(jnp.where(cur == -jnp.inf, 1.0, 0.0), axis=0, keepdims=True)
    return rank, top, jnp.abs(taken - float(PEER_TOPK))


def _staircase(top1, top2):
    nt = top1.shape[1]
    row = lax.broadcasted_iota(jnp.int32, (PEER_TOPK, nt), 0).astype(F32)
    cnt = jnp.zeros((PEER_TOPK, nt), F32)
    front = top1 + top2[0:1]
    for _ in range(PEER_TOPK):
        m = jnp.max(front, axis=0, keepdims=True)
        a = jnp.min(jnp.where(front == m, row, float(PEER_TOPK)), axis=0, keepdims=True)
        sel = row == a
        cnt = jnp.where(sel, cnt + 1.0, cnt)
        nxt = jnp.sum(jnp.where(sel, cnt, 0.0), axis=0, keepdims=True)
        val = jnp.sum(jnp.where(row == nxt, top2, 0.0), axis=0, keepdims=True)
        val = jnp.where(nxt >= float(PEER_TOPK), -jnp.inf, val)
        front = jnp.where(sel, top1 + val, front)
    return cnt


def _gelu(x):
    return 0.5 * x * (1.0 + lax.erf(x * (2.0 ** -0.5)))


def _peer_kernel(n3_ref, h2_ref, wqT_ref, sk_ref, u_ref, vT_ref, gf_ref, out_ref,
                 acc_ref, act_ref, coef_ref, qT_ref, s_ref, rb_ref, eb_ref, ci_ref, ea_ref):
    c = pl.program_id(1)
    nchunks = pl.num_programs(1)

    @pl.when(c == 0)
    def _prepare():
        acc_ref[...] = jnp.zeros_like(acc_ref)
        qT_ref[...] = _dot_nt(wqT_ref[...], n3_ref[...]).astype(BF16)
        for hc in range(2 * PEER_HEADS):
            s = _dot(sk_ref[hc], qT_ref[hc * PEER_HALF:(hc + 1) * PEER_HALF, :])
            for lc in range(PEER_LC):
                s_ref[hc, lc] = s[:, lc * LANES:(lc + 1) * LANES]

        def select(h, lc, exact):
            s1 = s_ref[2 * h, lc]
            s2 = s_ref[2 * h + 1, lc]
            if exact:
                rank1, top1 = _top16(s1)
                rank2, top2 = _top16(s2)
                tie = None
            else:
                _, top1, tie1 = _top16_distinct(s1, False)
                rank2, top2, tie2 = _top16_distinct(s2, True)
                tie = jnp.maximum(tie1, tie2)
            cnt = _staircase(top1, top2)
            m1 = top1[0:1]
            m2 = top2[0:1]
            e1 = jnp.exp(top1 - m1)
            e2 = jnp.exp(top2 - m2)
            pref = jnp.zeros_like(cnt)
            for b in range(PEER_TOPK):
                pref = pref + jnp.where(cnt > float(b), e2[b:b + 1], 0.0)
            z = jnp.sum(e1 * pref, axis=0, keepdims=True)
            ci = jnp.zeros_like(s1)
            for a in range(PEER_TOPK):
                in_row = (rank1 == float(a)) if exact else (s1 == top1[a:a + 1])
                ci = jnp.where(in_row, cnt[a:a + 1], ci)
            rb_ref[h, lc] = rank2.astype(BF16)
            eb_ref[h, lc] = jnp.exp(s2 - m2).astype(BF16)
            ci_ref[h, lc] = ci
            ea_ref[h, lc] = jnp.exp(s1 - m1) / z
            return tie

        def per_head(h, carry):
            tie = select(h, 0, False)
            for lc in range(1, PEER_LC):
                tie = jnp.maximum(tie, select(h, lc, False))

            @pl.when(jnp.max(tie) > 0.0)
            def _with_ties():
                def slow(lc, inner):
                    select(h, lc, True)
                    return inner

                lax.fori_loop(0, PEER_LC, slow, 0)

            return carry

        lax.fori_loop(0, PEER_HEADS, per_head, 0)

    act_ref[...] = _gelu(_dot_nt(u_ref[...], n3_ref[...])).astype(BF16)
    one = jnp.where(c >= 0, 1.0, 0.0).astype(BF16)
    i0 = pl.multiple_of(c * PEER_ROWS, PEER_ROWS)
    for lc in range(PEER_LC):
        ls = slice(lc * LANES, (lc + 1) * LANES)
        for k0 in range(0, PEER_KEYS, PEER_KSLAB):
            ks = slice(k0, k0 + PEER_KSLAB)
            gates = [jnp.zeros((PEER_KSLAB, LANES), BF16) for _ in range(PEER_ROWS)]
            for h in range(PEER_HEADS):
                rb = rb_ref[h, lc, ks, :] * one
                eb = eb_ref[h, lc, ks, :] * one
                ci8 = ci_ref[h, lc, pl.ds(i0, PEER_ROWS), :]
                ea8 = ea_ref[h, lc, pl.ds(i0, PEER_ROWS), :]
                for il in range(PEER_ROWS):
                    keep = rb < ci8[il:il + 1, :].astype(BF16)
                    w = eb * ea8[il:il + 1, :].astype(BF16)
                    gates[il] = gates[il] + jnp.where(keep, w, jnp.zeros_like(w))
            for il in range(PEER_ROWS):
                rs = slice(il * PEER_KEYS + k0, il * PEER_KEYS + k0 + PEER_KSLAB)
                coef_ref[rs, ls] = gates[il] * act_ref[rs, ls]
    acc_ref[...] += _dot(vT_ref[...], coef_ref[...])

    @pl.when(c == nchunks - 1)
    def _finish():
        out_ref[...] = _rms(h2_ref[...] + acc_ref[...].T, gf_ref[...])


def _peer(n3, h2, wqT, sk, u, vT, g_final):
    t, d = n3.shape
    nt = t // PEER_TOK
    nc = PEER_EXPERTS // PEER_ECHUNK
    tile4 = (PEER_HEADS, PEER_LC, PEER_KEYS, LANES)
    return pl.pallas_call(
        _peer_kernel,
        grid=(nt, nc),
        in_specs=[
            pl.BlockSpec((PEER_TOK, d), lambda i, c: (i, 0)),
            pl.BlockSpec((PEER_TOK, d), lambda i, c: (i, 0)),
            pl.BlockSpec((PEER_HEADS * PEER_QDIM, d), lambda i, c: (0, 0)),
            pl.BlockSpec((2 * PEER_HEADS, PEER_KEYS, PEER_HALF), lambda i, c: (0, 0, 0)),
            pl.BlockSpec((PEER_ECHUNK, d), lambda i, c: (c, 0)),
            pl.BlockSpec((d, PEER_ECHUNK), lambda i, c: (0, c)),
            pl.BlockSpec((1, d), lambda i, c: (0, 0)),
        ],
        out_specs=pl.BlockSpec((PEER_TOK, d), lambda i, c: (i, 0)),
        out_shape=jax.ShapeDtypeStruct((t, d), F32),
        scratch_shapes=[
            pltpu.VMEM((d, PEER_TOK), F32),
            pltpu.VMEM((PEER_ECHUNK, PEER_TOK), BF16),
            pltpu.VMEM((PEER_ECHUNK, PEER_TOK), BF16),
            pltpu.VMEM((PEER_HEADS * PEER_QDIM, PEER_TOK), BF16),
            pltpu.VMEM((2 * PEER_HEADS, PEER_LC, PEER_KEYS, LANES), F32),
            pltpu.VMEM(tile4, BF16),
            pltpu.VMEM(tile4, BF16),
            pltpu.VMEM(tile4, F32),
            pltpu.VMEM(tile4, F32),
        ],
        compiler_params=pltpu.CompilerParams(vmem_limit_bytes=VMEM_LIMIT),
        name="peer",
    )(n3, h2, wqT, sk, u, vT, g_final.reshape(1, d))


def kernel(x, mem, g_mix, w_in, conv_w, g_sb_out, g_conv_out, w_out, g_xattn, g_mem, w_q_mem, w_kv_mem, w_o_mem, g_ffn, w_query, sub_keys, expert_u, expert_v, g_final):
    b, s, d = x.shape
    assert g_mix.shape[0] == 1, "the final rmsnorm is fused into the single layer's PEER kernel"
    l = 0
    qT, k, vT, gb, cx = _in_proj(x, g_mix[l], w_in[l].astype(BF16))
    sbT = _attention(qT, k, vT)
    kmem, vmem = _mem_kv(mem, g_mem[l], w_kv_mem[l].astype(BF16))
    h2, n3 = _post_mixer(sbT, gb, cx, x, kmem, vmem, conv_w[l], g_sb_out[l], g_conv_out[l],
                         w_out[l].astype(BF16), g_xattn[l], w_q_mem[l].astype(BF16),
                         w_o_mem[l].astype(BF16), g_ffn[l])
    out = _peer(n3.reshape(b * s, d), h2.reshape(b * s, d),
                w_query[l].T.astype(BF16),
                sub_keys[l].reshape(2 * PEER_HEADS, PEER_KEYS, PEER_HALF).astype(BF16),
                expert_u[l].astype(BF16), expert_v[l].T.astype(BF16), g_final)
    return out.reshape(b, s, d)
```

```python
import functools
import math

import jax
import jax.numpy as jnp
from jax import lax
from jax.experimental import pallas as pl
from jax.experimental.pallas import tpu as pltpu

F32 = jnp.float32
BF16 = jnp.bfloat16

D_MODEL = 1024
EPS = 1e-6
SB_HEADS = 8
SB_HEAD_DIM = 64
SB_WIDTH = SB_HEADS * SB_HEAD_DIM
CONV_CH = D_MODEL - SB_WIDTH
CONV_K = 3
IN_COLS = 3 * SB_WIDTH + 3 * CONV_CH
MEM_HEADS = 4
MEM_HEAD_DIM = D_MODEL // MEM_HEADS
PEER_HEADS = 8
PEER_KEYS = 128
PEER_EXPERTS = PEER_KEYS * PEER_KEYS
PEER_QDIM = 256
PEER_HALF = PEER_QDIM // 2
PEER_TOPK = 16

LANES = 128
SUBLANES = 8

TOK_TILE = 512
ATT_BLOCK = 256
ATT_UNROLL = 4
HEADS_PER_STEP = LANES // SB_HEAD_DIM
PEER_TOK = 512
PEER_LC = PEER_TOK // LANES
PEER_ECHUNK = 1024
PEER_ROWS = PEER_ECHUNK // PEER_KEYS
PEER_KSLAB = 64
VMEM_LIMIT = 58 * 1024 * 1024
LOG2E = math.log2(math.e)
EXP2_CLAMP = 126.0


def _rms(x, g):
    return x * lax.rsqrt(jnp.mean(x * x, axis=-1, keepdims=True) + EPS) * g


def _dot(a, b):
    return jnp.dot(a, b, preferred_element_type=F32)


def _dot_nt(a, b):
    return lax.dot_general(a, b, (((1,), (1,)), ((), ())), preferred_element_type=F32)


def _inproj_kernel(x_ref, g_ref, w_ref, qT_ref, k_ref, vT_ref, gb_ref, cx_ref):
    n = _rms(x_ref[0], g_ref[...]).astype(BF16)
    proj = _dot(n, w_ref[...])
    q = proj[:, 0:SB_WIDTH] * (SB_HEAD_DIM ** -0.5 * LOG2E)
    qT_ref[0] = q.T.astype(BF16)
    k_ref[0] = proj[:, SB_WIDTH:2 * SB_WIDTH].astype(BF16)
    vT = proj[:, 2 * SB_WIDTH:3 * SB_WIDTH].T.astype(BF16)
    for j in range(TOK_TILE // ATT_BLOCK):
        vT_ref[0, j] = vT[:, j * ATT_BLOCK:(j + 1) * ATT_BLOCK]
    o = 3 * SB_WIDTH
    gb_ref[0] = proj[:, o:o + CONV_CH]
    cx_ref[0] = proj[:, o + CONV_CH:o + 2 * CONV_CH] * proj[:, o + 2 * CONV_CH:o + 3 * CONV_CH]


def _in_proj(x, g_mix, w_in):
    b, s, d = x.shape
    nt = s // TOK_TILE
    return pl.pallas_call(
        _inproj_kernel,
        grid=(b, nt),
        in_specs=[
            pl.BlockSpec((1, TOK_TILE, d), lambda bi, i: (bi, i, 0)),
            pl.BlockSpec((1, d), lambda bi, i: (0, 0)),
            pl.BlockSpec((d, IN_COLS), lambda bi, i: (0, 0)),
        ],
        out_specs=[
            pl.BlockSpec((1, SB_WIDTH, TOK_TILE), lambda bi, i: (bi, 0, i)),
            pl.BlockSpec((1, TOK_TILE, SB_WIDTH), lambda bi, i: (bi, i, 0)),
            pl.BlockSpec((1, TOK_TILE // ATT_BLOCK, SB_WIDTH, ATT_BLOCK), lambda bi, i: (bi, i, 0, 0)),
            pl.BlockSpec((1, TOK_TILE, CONV_CH), lambda bi, i: (bi, i, 0)),
            pl.BlockSpec((1, TOK_TILE, CONV_CH), lambda bi, i: (bi, i, 0)),
        ],
        out_shape=[
            jax.ShapeDtypeStruct((b, SB_WIDTH, s), BF16),
            jax.ShapeDtypeStruct((b, s, SB_WIDTH), BF16),
            jax.ShapeDtypeStruct((b, s // ATT_BLOCK, SB_WIDTH, ATT_BLOCK), BF16),
            jax.ShapeDtypeStruct((b, s, CONV_CH), F32),
            jax.ShapeDtypeStruct((b, s, CONV_CH), F32),
        ],
        compiler_params=pltpu.CompilerParams(vmem_limit_bytes=VMEM_LIMIT),
        name="in_proj",
    )(x, g_mix.reshape(1, d), w_in)


def _attn_kernel(qT_ref, k_ref, vT_ref, u_ref, o_ref, qm_ref, acc_ref, c_ref, s_ref,
                 z_ref, n_ref, a_ref):
    qi = pl.program_id(1)
    tb = ATT_BLOCK
    feat = lax.broadcasted_iota(jnp.int32, (LANES, tb), 0)
    for h in range(SB_HEADS):
        p, hh = divmod(h, HEADS_PER_STEP)
        slab = qT_ref[0, p * LANES:(p + 1) * LANES, :]
        in_head = (feat >= hh * SB_HEAD_DIM) & (feat < (hh + 1) * SB_HEAD_DIM)
        qm_ref[h] = jnp.where(in_head, slab, jnp.zeros_like(slab))
    acc_ref[...] = jnp.zeros_like(acc_ref)
    c_ref[...] = jnp.zeros_like(c_ref)
    key_pos = lax.broadcasted_iota(jnp.int32, (tb, tb), 0)
    qry_pos = lax.broadcasted_iota(jnp.int32, (tb, tb), 1)
    past = key_pos < qry_pos

    def blocks(jbs, diagonal):
        for bank, jb in enumerate(jbs):
            r0 = pl.multiple_of(jb * tb, tb)
            for p in range(SB_WIDTH // LANES):
                kb = k_ref[0, pl.ds(r0, tb), p * LANES:(p + 1) * LANES]
                for hh in range(HEADS_PER_STEP):
                    h = p * HEADS_PER_STEP + hh
                    z = _dot(kb, qm_ref[h])
                    n = jnp.maximum(z, jnp.log(1.0 + jnp.exp2(jnp.minimum(z, EXP2_CLAMP))) * LOG2E)
                    if diagonal:
                        n = jnp.where(past, n, 0.0)
                    z_ref[bank, h] = z
                    n_ref[bank, h] = n.astype(BF16)
        for bank, jb in enumerate(jbs):
            for h in range(SB_HEADS):
                cum = _dot(u_ref[...], n_ref[bank, h])
                a = jnp.exp2(z_ref[bank, h] - cum)
                if diagonal:
                    a = jnp.where(past, a, 0.0)
                a_ref[bank, h] = a.astype(BF16)
                s_ref[bank, h:h + 1, :] = cum[0:1, :]
        for bank, jb in enumerate(jbs):
            for h in range(SB_HEADS):
                vb = vT_ref[0, jb, h * SB_HEAD_DIM:(h + 1) * SB_HEAD_DIM, :]
                c = c_ref[h:h + 1, :]
                acc_ref[h] += jnp.exp2(-c) * _dot(vb, a_ref[bank, h])
                c_ref[h:h + 1, :] = c + s_ref[bank, h:h + 1, :]

    blocks([qi], True)

    def body(j, carry):
        top = qi - 1 - ATT_UNROLL * j
        blocks([top - u for u in range(ATT_UNROLL)], False)
        return carry

    lax.fori_loop(0, qi // ATT_UNROLL, body, 0)
    rem = qi % ATT_UNROLL
    size = ATT_UNROLL // 2
    while size >= 1:
        @pl.when((rem & size) != 0)
        def _tail(size=size):
            top = (rem & (2 * size - 1)) - 1
            blocks([top - u for u in range(size)], False)
        size //= 2

    for h in range(SB_HEADS):
        o_ref[0, h * SB_HEAD_DIM:(h + 1) * SB_HEAD_DIM, :] = acc_ref[h]


def _attention(qT, k, vT):
    b, _, s = qT.shape
    nq = s // ATT_BLOCK
    tri = (lax.broadcasted_iota(jnp.int32, (ATT_BLOCK, ATT_BLOCK), 1)
           >= lax.broadcasted_iota(jnp.int32, (ATT_BLOCK, ATT_BLOCK), 0)).astype(BF16)
    return pl.pallas_call(
        _attn_kernel,
        grid=(b, nq),
        in_specs=[
            pl.BlockSpec((1, SB_WIDTH, ATT_BLOCK), lambda bi, q: (bi, 0, q)),
            pl.BlockSpec((1, s, SB_WIDTH), lambda bi, q: (bi, 0, 0)),
            pl.BlockSpec((1, nq, SB_WIDTH, ATT_BLOCK), lambda bi, q: (bi, 0, 0, 0)),
            pl.BlockSpec((ATT_BLOCK, ATT_BLOCK), lambda bi, q: (0, 0)),
        ],
        out_specs=pl.BlockSpec((1, SB_WIDTH, ATT_BLOCK), lambda bi, q: (bi, 0, q)),
        out_shape=jax.ShapeDtypeStruct((b, SB_WIDTH, s), F32),
        scratch_shapes=[
            pltpu.VMEM((SB_HEADS, LANES, ATT_BLOCK), BF16),
            pltpu.VMEM((SB_HEADS, SB_HEAD_DIM, ATT_BLOCK), F32),
            pltpu.VMEM((SB_HEADS, ATT_BLOCK), F32),
            pltpu.VMEM((ATT_UNROLL, SB_HEADS, ATT_BLOCK), F32),
            pltpu.VMEM((ATT_UNROLL, SB_HEADS, ATT_BLOCK, ATT_BLOCK), F32),
            pltpu.VMEM((ATT_UNROLL, SB_HEADS, ATT_BLOCK, ATT_BLOCK), BF16),
            pltpu.VMEM((ATT_UNROLL, SB_HEADS, ATT_BLOCK, ATT_BLOCK), BF16),
        ],
        compiler_params=pltpu.CompilerParams(vmem_limit_bytes=VMEM_LIMIT),
        name="sb_attention",
    )(qT, k, vT, tri)


def _memkv_kernel(mem_ref, g_ref, w_ref, k_ref, v_ref):
    mn = _rms(mem_ref[0], g_ref[...]).astype(BF16)
    kv = _dot(mn, w_ref[...])
    k_ref[0] = kv[:, :D_MODEL].astype(BF16)
    v_ref[0] = kv[:, D_MODEL:].astype(BF16)


def _mem_kv(mem, g_mem, w_kv):
    b, m, d = mem.shape
    return pl.pallas_call(
        _memkv_kernel,
        grid=(b,),
        in_specs=[
            pl.BlockSpec((1, m, d), lambda bi: (bi, 0, 0)),
            pl.BlockSpec((1, d), lambda bi: (0, 0)),
            pl.BlockSpec((d, 2 * d), lambda bi: (0, 0)),
        ],
        out_specs=[pl.BlockSpec((1, m, d), lambda bi: (bi, 0, 0))] * 2,
        out_shape=[jax.ShapeDtypeStruct((b, m, d), BF16)] * 2,
        compiler_params=pltpu.CompilerParams(vmem_limit_bytes=VMEM_LIMIT),
        name="mem_kv",
    )(mem, g_mem.reshape(1, d), w_kv)


def _post_kernel(sbT_ref, gb_ref, cx_ref, halo_ref, x_ref, km_ref, vm_ref, cw_ref,
                 gsb_ref, gcv_ref, wout_ref, gx_ref, wq_ref, wo_ref, gffn_ref,
                 h2_ref, n3_ref):
    i = pl.program_id(1)
    tm = TOK_TILE
    sbn = _rms(sbT_ref[0].T, gsb_ref[...])

    cx = cx_ref[0]
    halo = jnp.where(i > 0, halo_ref[0], 0.0)
    row = lax.broadcasted_iota(jnp.int32, (tm, CONV_CH), 0)
    cx1 = jnp.where(row == 0, halo[7:8], pltpu.roll(cx, 1, 0))
    cx2 = jnp.where(row == 0, halo[6:7],
                    jnp.where(row == 1, halo[7:8], pltpu.roll(cx, 2, 0)))
    cw = cw_ref[...]
    conv = gb_ref[0] * (cw[0:1] * cx2 + cw[1:2] * cx1 + cw[2:3] * cx)
    cvn = _rms(conv, gcv_ref[...])

    h1 = (x_ref[0] + _dot(sbn.astype(BF16), wout_ref[0:SB_WIDTH, :])
          + _dot(cvn.astype(BF16), wout_ref[SB_WIDTH:, :]))

    n2 = _rms(h1, gx_ref[...]).astype(BF16)
    qm = _dot(n2, wq_ref[...]) * (MEM_HEAD_DIM ** -0.5)
    heads = []
    for h in range(MEM_HEADS):
        sl = slice(h * MEM_HEAD_DIM, (h + 1) * MEM_HEAD_DIM)
        sc = _dot_nt(qm[:, sl].astype(BF16), km_ref[0, :, sl])
        p = jnp.exp(sc - jnp.max(sc, axis=-1, keepdims=True))
        p = p / jnp.sum(p, axis=-1, keepdims=True)
        heads.append(_dot(p.astype(BF16), vm_ref[0, :, sl]).astype(BF16))
    h2 = h1 + _dot(jnp.concatenate(heads, axis=1), wo_ref[...])
    h2_ref[0] = h2
    n3_ref[0] = _rms(h2, gffn_ref[...]).astype(BF16)


def _post_mixer(sbT, gb, cx, x, kmem, vmem, conv_w, g_sb, g_cv, w_out, g_x, w_q, w_o, g_ffn):
    b, s, d = x.shape
    nt = s // TOK_TILE
    m = kmem.shape[1]
    hb = TOK_TILE // SUBLANES
    const = lambda shape: pl.BlockSpec(shape, lambda bi, i: (0,) * len(shape))
    return pl.pallas_call(
        _post_kernel,
        grid=(b, nt),
        in_specs=[
            pl.BlockSpec((1, SB_WIDTH, TOK_TILE), lambda bi, i: (bi, 0, i)),
            pl.BlockSpec((1, TOK_TILE, CONV_CH), lambda bi, i: (bi, i, 0)),
            pl.BlockSpec((1, TOK_TILE, CONV_CH), lambda bi, i: (bi, i, 0)),
            pl.BlockSpec((1, SUBLANES, CONV_CH), lambda bi, i: (bi, jnp.maximum(i * hb - 1, 0), 0)),
            pl.BlockSpec((1, TOK_TILE, d), lambda bi, i: (bi, i, 0)),
            pl.BlockSpec((1, m, d), lambda bi, i: (bi, 0, 0)),
            pl.BlockSpec((1, m, d), lambda bi, i: (bi, 0, 0)),
            const((CONV_K, CONV_CH)),
            const((1, SB_WIDTH)),
            const((1, CONV_CH)),
            const((d, d)),
            const((1, d)),
            const((d, d)),
            const((d, d)),
            const((1, d)),
        ],
        out_specs=[
            pl.BlockSpec((1, TOK_TILE, d), lambda bi, i: (bi, i, 0)),
            pl.BlockSpec((1, TOK_TILE, d), lambda bi, i: (bi, i, 0)),
        ],
        out_shape=[
            jax.ShapeDtypeStruct((b, s, d), F32),
            jax.ShapeDtypeStruct((b, s, d), BF16),
        ],
        compiler_params=pltpu.CompilerParams(vmem_limit_bytes=VMEM_LIMIT),
        name="post_mixer",
    )(sbT, gb, cx, cx, x, kmem, vmem, conv_w, g_sb.reshape(1, -1), g_cv.reshape(1, -1), w_out,
      g_x.reshape(1, d), w_q, w_o, g_ffn.reshape(1, d))


def _top16(s):
    nk, nt = s.shape
    key_id = lax.broadcasted_iota(jnp.int32, (nk, nt), 0).astype(F32)
    slot = lax.broadcasted_iota(jnp.int32, (PEER_TOPK, nt), 0)
    cur = s
    rank = jnp.full((nk, nt), float(PEER_TOPK), F32)
    top = jnp.zeros((PEER_TOPK, nt), F32)
    for r in range(PEER_TOPK):
        m = jnp.max(cur, axis=0, keepdims=True)
        first = jnp.min(jnp.where(cur == m, key_id, float(nk)), axis=0, keepdims=True)
        sel = key_id == first
        rank = jnp.where(sel, float(r), rank)
        cur = jnp.where(sel, -jnp.inf, cur)
        top = jnp.where(slot == r, m, top)
    return rank, top


def _top16_distinct(s, want_rank):
    nk, nt = s.shape
    slot = lax.broadcasted_iota(jnp.int32, (PEER_TOPK, nt), 0)
    cur = s
    rank = jnp.full((nk, nt), float(PEER_TOPK), F32)
    top = jnp.zeros((PEER_TOPK, nt), F32)
    for r in range(PEER_TOPK):
        m = jnp.max(cur, axis=0, keepdims=True)
        hit = cur == m
        if want_rank:
            rank = jnp.where(hit, float(r), rank)
        cur = jnp.where(hit, -jnp.inf, cur)
        top = jnp.where(slot == r, m, top)
    taken = jnp.sum(jnp.where(cur == -jnp.inf, 1.0, 0.0), axis=0, keepdims=True)
    return rank, top, jnp.abs(taken - float(PEER_TOPK))


def _staircase(top1, top2):
    nt = top1.shape[1]
    row = lax.broadcasted_iota(jnp.int32, (PEER_TOPK, nt), 0).astype(F32)
    cnt = jnp.zeros((PEER_TOPK, nt), F32)
    front = top1 + top2[0:1]
    for _ in range(PEER_TOPK):
        m = jnp.max(front, axis=0, keepdims=True)
        a = jnp.min(jnp.where(front == m, row, float(PEER_TOPK)), axis=0, keepdims=True)
        sel = row == a
        cnt = jnp.where(sel, cnt + 1.0, cnt)
        nxt = jnp.sum(jnp.where(sel, cnt, 0.0), axis=0, keepdims=True)
        val = jnp.sum(jnp.where(row == nxt, top2, 0.0), axis=0, keepdims=True)
        val = jnp.where(nxt >= float(PEER_TOPK), -jnp.inf, val)
        front = jnp.where(sel, top1 + val, front)
    return cnt


def _gelu(x):
    return 0.5 * x * (1.0 + lax.erf(x * (2.0 ** -0.5)))


def _peer_kernel(n3_ref, h2_ref, wqT_ref, sk_ref, u0_ref, u_ref, vT_ref, gf_ref, out_ref,
                 acc_ref, act_ref, coef_ref, qT_ref, s_ref, rb_ref, eb_ref, ci_ref, ea_ref):
    c = pl.program_id(1)
    nchunks = pl.num_programs(1) - 1

    @pl.when(c == 0)
    def _prepare():
        acc_ref[...] = jnp.zeros_like(acc_ref)
        qT_ref[...] = _dot_nt(wqT_ref[...], n3_ref[...]).astype(BF16)
        for hc in range(2 * PEER_HEADS):
            s = _dot(sk_ref[hc], qT_ref[hc * PEER_HALF:(hc + 1) * PEER_HALF, :])
            for lc in range(PEER_LC):
                s_ref[hc, lc] = s[:, lc * LANES:(lc + 1) * LANES]

        def select(h, lc, exact):
            s1 = s_ref[2 * h, lc]
            s2 = s_ref[2 * h + 1, lc]
            if exact:
                rank1, top1 = _top16(s1)
                rank2, top2 = _top16(s2)
                tie = None
            else:
                _, top1, tie1 = _top16_distinct(s1, False)
                rank2, top2, tie2 = _top16_distinct(s2, True)
                tie = jnp.maximum(tie1, tie2)
            cnt = _staircase(top1, top2)
            m1 = top1[0:1]
            m2 = top2[0:1]
            e1 = jnp.exp(top1 - m1)
            e2 = jnp.exp(top2 - m2)
            pref = jnp.zeros_like(cnt)
            for b in range(PEER_TOPK):
                pref = pref + jnp.where(cnt > float(b), e2[b:b + 1], 0.0)
            z = jnp.sum(e1 * pref, axis=0, keepdims=True)
            ci = jnp.zeros_like(s1)
            for a in range(PEER_TOPK):
                in_row = (rank1 == float(a)) if exact else (s1 == top1[a:a + 1])
                ci = jnp.where(in_row, cnt[a:a + 1], ci)
            rb_ref[h, lc] = rank2.astype(BF16)
            eb_ref[h, lc] = jnp.exp(s2 - m2).astype(BF16)
            ci_ref[h, lc] = ci
            ea_ref[h, lc] = jnp.exp(s1 - m1) / z
            return tie

        def per_head(h, carry):
            tie = select(h, 0, False)
            for lc in range(1, PEER_LC):
                tie = jnp.maximum(tie, select(h, lc, False))

            @pl.when(jnp.max(tie) > 0.0)
            def _with_ties():
                def slow(lc, inner):
                    select(h, lc, True)
                    return inner

                lax.fori_loop(0, PEER_LC, slow, 0)

            return carry

        lax.fori_loop(0, PEER_HEADS, per_head, 0)

    cur = c % 2
    nxt = 1 - cur

    def activations(u_rows_ref):
        return _gelu(_dot_nt(u_rows_ref[...], n3_ref[...])).astype(BF16)

    @pl.when(c == 0)
    def _first():
        coef_ref[1] = jnp.zeros((PEER_ECHUNK, PEER_TOK), BF16)
        act_ref[0] = activations(u0_ref)

    @pl.when(c < nchunks)
    def _chunk():
        one = jnp.where(c >= 0, 1.0, 0.0).astype(BF16)
        i0 = pl.multiple_of(c * PEER_ROWS, PEER_ROWS)
        for lc in range(PEER_LC):
            ls = slice(lc * LANES, (lc + 1) * LANES)
            ur = slice(lc * PEER_ECHUNK // PEER_LC, (lc + 1) * PEER_ECHUNK // PEER_LC)
            dr = slice(lc * D_MODEL // PEER_LC, (lc + 1) * D_MODEL // PEER_LC)
            act_ref[nxt, ur, :] = activations(u_ref.at[ur, :])
            acc_ref[dr, :] += _dot(vT_ref[dr, :], coef_ref[nxt])
            for k0 in range(0, PEER_KEYS, PEER_KSLAB):
                ks = slice(k0, k0 + PEER_KSLAB)
                gates = [jnp.zeros((PEER_KSLAB, LANES), BF16) for _ in range(PEER_ROWS)]
                for h in range(PEER_HEADS):
                    rb = rb_ref[h, lc, ks, :] * one
                    eb = eb_ref[h, lc, ks, :] * one
                    ci8 = ci_ref[h, lc, pl.ds(i0, PEER_ROWS), :]
                    ea8 = ea_ref[h, lc, pl.ds(i0, PEER_ROWS), :]
                    for il in range(PEER_ROWS):
                        keep = rb < ci8[il:il + 1, :].astype(BF16)
                        w = eb * ea8[il:il + 1, :].astype(BF16)
                        gates[il] = gates[il] + jnp.where(keep, w, jnp.zeros_like(w))
                for il in range(PEER_ROWS):
                    rs = slice(il * PEER_KEYS + k0, il * PEER_KEYS + k0 + PEER_KSLAB)
                    coef_ref[cur, rs, ls] = gates[il] * act_ref[cur, rs, ls]

    @pl.when(c == nchunks)
    def _finish():
        acc = acc_ref[...] + _dot(vT_ref[...], coef_ref[nxt])
        out_ref[...] = _rms(h2_ref[...] + acc.T, gf_ref[...])


def _peer(n3, h2, wqT, sk, u, vT, g_final):
    t, d = n3.shape
    nt = t // PEER_TOK
    nc = PEER_EXPERTS // PEER_ECHUNK
    tile4 = (PEER_HEADS, PEER_LC, PEER_KEYS, LANES)
    return pl.pallas_call(
        _peer_kernel,
        grid=(nt, nc + 1),
        in_specs=[
            pl.BlockSpec((PEER_TOK, d), lambda i, c: (i, 0)),
            pl.BlockSpec((PEER_TOK, d), lambda i, c: (i, 0)),
            pl.BlockSpec((PEER_HEADS * PEER_QDIM, d), lambda i, c: (0, 0)),
            pl.BlockSpec((2 * PEER_HEADS, PEER_KEYS, PEER_HALF), lambda i, c: (0, 0, 0)),
            pl.BlockSpec((PEER_ECHUNK, d), lambda i, c: (0, 0)),
            pl.BlockSpec((PEER_ECHUNK, d), lambda i, c: (jnp.minimum(c + 1, nc - 1), 0)),
            pl.BlockSpec((d, PEER_ECHUNK), lambda i, c: (0, jnp.maximum(c - 1, 0))),
            pl.BlockSpec((1, d), lambda i, c: (0, 0)),
        ],
        out_specs=pl.BlockSpec((PEER_TOK, d), lambda i, c: (i, 0)),
        out_shape=jax.ShapeDtypeStruct((t, d), F32),
        scratch_shapes=[
            pltpu.VMEM((d, PEER_TOK), F32),
            pltpu.VMEM((2, PEER_ECHUNK, PEER_TOK), BF16),
            pltpu.VMEM((2, PEER_ECHUNK, PEER_TOK), BF16),
            pltpu.VMEM((PEER_HEADS * PEER_QDIM, PEER_TOK), BF16),
            pltpu.VMEM((2 * PEER_HEADS, PEER_LC, PEER_KEYS, LANES), F32),
            pltpu.VMEM(tile4, BF16),
            pltpu.VMEM(tile4, BF16),
            pltpu.VMEM(tile4, F32),
            pltpu.VMEM(tile4, F32),
        ],
        compiler_params=pltpu.CompilerParams(vmem_limit_bytes=VMEM_LIMIT),
        name="peer",
    )(n3, h2, wqT, sk, u, u, vT, g_final.reshape(1, d))


def kernel(x, mem, g_mix, w_in, conv_w, g_sb_out, g_conv_out, w_out, g_xattn, g_mem, w_q_mem, w_kv_mem, w_o_mem, g_ffn, w_query, sub_keys, expert_u, expert_v, g_final):
    b, s, d = x.shape
    assert g_mix.shape[0] == 1, "the final rmsnorm is fused into the single layer's PEER kernel"
    l = 0
    qT, k, vT, gb, cx = _in_proj(x, g_mix[l], w_in[l].astype(BF16))
    sbT = _attention(qT, k, vT)
    kmem, vmem = _mem_kv(mem, g_mem[l], w_kv_mem[l].astype(BF16))
    h2, n3 = _post_mixer(sbT, gb, cx, x, kmem, vmem, conv_w[l], g_sb_out[l], g_conv_out[l],
                         w_out[l].astype(BF16), g_xattn[l], w_q_mem[l].astype(BF16),
                         w_o_mem[l].astype(BF16), g_ffn[l])
    out = _peer(n3.reshape(b * s, d), h2.reshape(b * s, d),
                w_query[l].T.astype(BF16),
                sub_keys[l].reshape(2 * PEER_HEADS, PEER_KEYS, PEER_HALF).astype(BF16),
                expert_u[l].astype(BF16), expert_v[l].T.astype(BF16), g_final)
    return out.reshape(b, s, d)
```

```python
import functools
import math

import jax
import jax.numpy as jnp
from jax import lax
from jax.experimental import pallas as pl
from jax.experimental.pallas import tpu as pltpu

F32 = jnp.float32
BF16 = jnp.bfloat16

D_MODEL = 1024
EPS = 1e-6
SB_HEADS = 8
SB_HEAD_DIM = 64
SB_WIDTH = SB_HEADS * SB_HEAD_DIM
CONV_CH = D_MODEL - SB_WIDTH
CONV_K = 3
IN_COLS = 3 * SB_WIDTH + 3 * CONV_CH
MEM_HEADS = 4
MEM_HEAD_DIM = D_MODEL // MEM_HEADS
PEER_HEADS = 8
PEER_KEYS = 128
PEER_EXPERTS = PEER_KEYS * PEER_KEYS
PEER_QDIM = 256
PEER_HALF = PEER_QDIM // 2
PEER_TOPK = 16

LANES = 128
SUBLANES = 8

TOK_TILE = 512
ATT_BLOCK = 256
ATT_UNROLL = 4
HEADS_PER_STEP = LANES // SB_HEAD_DIM
PEER_TOK = 512
PEER_LC = PEER_TOK // LANES
PEER_ECHUNK = 1024
PEER_ROWS = PEER_ECHUNK // PEER_KEYS
PEER_KSLAB = 64
VMEM_LIMIT = 58 * 1024 * 1024
LOG2E = math.log2(math.e)
EXP2_CLAMP = 126.0


def _rms(x, g):
    return x * lax.rsqrt(jnp.mean(x * x, axis=-1, keepdims=True) + EPS) * g


def _dot(a, b):
    return jnp.dot(a, b, preferred_element_type=F32)


def _dot_nt(a, b):
    return lax.dot_general(a, b, (((1,), (1,)), ((), ())), preferred_element_type=F32)


def _inproj_kernel(x_ref, g_ref, w_ref, qT_ref, k_ref, vT_ref, gb_ref, cx_ref):
    n = _rms(x_ref[0], g_ref[...]).astype(BF16)
    proj = _dot(n, w_ref[...])
    q = proj[:, 0:SB_WIDTH] * (SB_HEAD_DIM ** -0.5 * LOG2E)
    qT_ref[0] = q.T.astype(BF16)
    k_ref[0] = proj[:, SB_WIDTH:2 * SB_WIDTH].astype(BF16)
    vT = proj[:, 2 * SB_WIDTH:3 * SB_WIDTH].T.astype(BF16)
    for j in range(TOK_TILE // ATT_BLOCK):
        vT_ref[0, j] = vT[:, j * ATT_BLOCK:(j + 1) * ATT_BLOCK]
    o = 3 * SB_WIDTH
    gb_ref[0] = proj[:, o:o + CONV_CH]
    cx_ref[0] = proj[:, o + CONV_CH:o + 2 * CONV_CH] * proj[:, o + 2 * CONV_CH:o + 3 * CONV_CH]


def _in_proj(x, g_mix, w_in):
    b, s, d = x.shape
    nt = s // TOK_TILE
    return pl.pallas_call(
        _inproj_kernel,
        grid=(b, nt),
        in_specs=[
            pl.BlockSpec((1, TOK_TILE, d), lambda bi, i: (bi, i, 0)),
            pl.BlockSpec((1, d), lambda bi, i: (0, 0)),
            pl.BlockSpec((d, IN_COLS), lambda bi, i: (0, 0)),
        ],
        out_specs=[
            pl.BlockSpec((1, SB_WIDTH, TOK_TILE), lambda bi, i: (bi, 0, i)),
            pl.BlockSpec((1, TOK_TILE, SB_WIDTH), lambda bi, i: (bi, i, 0)),
            pl.BlockSpec((1, TOK_TILE // ATT_BLOCK, SB_WIDTH, ATT_BLOCK), lambda bi, i: (bi, i, 0, 0)),
            pl.BlockSpec((1, TOK_TILE, CONV_CH), lambda bi, i: (bi, i, 0)),
            pl.BlockSpec((1, TOK_TILE, CONV_CH), lambda bi, i: (bi, i, 0)),
        ],
        out_shape=[
            jax.ShapeDtypeStruct((b, SB_WIDTH, s), BF16),
            jax.ShapeDtypeStruct((b, s, SB_WIDTH), BF16),
            jax.ShapeDtypeStruct((b, s // ATT_BLOCK, SB_WIDTH, ATT_BLOCK), BF16),
            jax.ShapeDtypeStruct((b, s, CONV_CH), F32),
            jax.ShapeDtypeStruct((b, s, CONV_CH), F32),
        ],
        compiler_params=pltpu.CompilerParams(vmem_limit_bytes=VMEM_LIMIT),
        name="in_proj",
    )(x, g_mix.reshape(1, d), w_in)


def _attn_kernel(qT_ref, k_ref, vT_ref, u_ref, o_ref, qm_ref, acc_ref, c_ref, s_ref,
                 z_ref, n_ref, a_ref):
    qi = pl.program_id(1)
    tb = ATT_BLOCK
    feat = lax.broadcasted_iota(jnp.int32, (LANES, tb), 0)
    for h in range(SB_HEADS):
        p, hh = divmod(h, HEADS_PER_STEP)
        slab = qT_ref[0, p * LANES:(p + 1) * LANES, :]
        in_head = (feat >= hh * SB_HEAD_DIM) & (feat < (hh + 1) * SB_HEAD_DIM)
        qm_ref[h] = jnp.where(in_head, slab, jnp.zeros_like(slab))
    acc_ref[...] = jnp.zeros_like(acc_ref)
    c_ref[...] = jnp.zeros_like(c_ref)
    key_pos = lax.broadcasted_iota(jnp.int32, (tb, tb), 0)
    qry_pos = lax.broadcasted_iota(jnp.int32, (tb, tb), 1)
    past = key_pos < qry_pos

    def blocks(jbs, diagonal):
        for bank, jb in enumerate(jbs):
            r0 = pl.multiple_of(jb * tb, tb)
            for p in range(SB_WIDTH // LANES):
                kb = k_ref[0, pl.ds(r0, tb), p * LANES:(p + 1) * LANES]
                for hh in range(HEADS_PER_STEP):
                    h = p * HEADS_PER_STEP + hh
                    z = _dot(kb, qm_ref[h])
                    n = jnp.maximum(z, jnp.log(1.0 + jnp.exp2(jnp.minimum(z, EXP2_CLAMP))) * LOG2E)
                    if diagonal:
                        n = jnp.where(past, n, 0.0)
                    z_ref[bank, h] = z
                    n_ref[bank, h] = n.astype(BF16)
        for bank, jb in enumerate(jbs):
            for h in range(SB_HEADS):
                cum = _dot(u_ref[...], n_ref[bank, h])
                a = jnp.exp2(z_ref[bank, h] - cum)
                if diagonal:
                    a = jnp.where(past, a, 0.0)
                a_ref[bank, h] = a.astype(BF16)
                s_ref[bank, h:h + 1, :] = cum[0:1, :]
        for bank, jb in enumerate(jbs):
            for h in range(SB_HEADS):
                vb = vT_ref[0, jb, h * SB_HEAD_DIM:(h + 1) * SB_HEAD_DIM, :]
                c = c_ref[h:h + 1, :]
                acc_ref[h] += jnp.exp2(-c) * _dot(vb, a_ref[bank, h])
                c_ref[h:h + 1, :] = c + s_ref[bank, h:h + 1, :]

    blocks([qi], True)

    def body(j, carry):
        top = qi - 1 - ATT_UNROLL * j
        blocks([top - u for u in range(ATT_UNROLL)], False)
        return carry

    lax.fori_loop(0, qi // ATT_UNROLL, body, 0)
    rem = qi % ATT_UNROLL
    size = ATT_UNROLL // 2
    while size >= 1:
        @pl.when((rem & size) != 0)
        def _tail(size=size):
            top = (rem & (2 * size - 1)) - 1
            blocks([top - u for u in range(size)], False)
        size //= 2

    for h in range(SB_HEADS):
        o_ref[0, h * SB_HEAD_DIM:(h + 1) * SB_HEAD_DIM, :] = acc_ref[h]


def _attention(qT, k, vT):
    b, _, s = qT.shape
    nq = s // ATT_BLOCK
    tri = (lax.broadcasted_iota(jnp.int32, (ATT_BLOCK, ATT_BLOCK), 1)
           >= lax.broadcasted_iota(jnp.int32, (ATT_BLOCK, ATT_BLOCK), 0)).astype(BF16)
    return pl.pallas_call(
        _attn_kernel,
        grid=(b, nq),
        in_specs=[
            pl.BlockSpec((1, SB_WIDTH, ATT_BLOCK), lambda bi, q: (bi, 0, q)),
            pl.BlockSpec((1, s, SB_WIDTH), lambda bi, q: (bi, 0, 0)),
            pl.BlockSpec((1, nq, SB_WIDTH, ATT_BLOCK), lambda bi, q: (bi, 0, 0, 0)),
            pl.BlockSpec((ATT_BLOCK, ATT_BLOCK), lambda bi, q: (0, 0)),
        ],
        out_specs=pl.BlockSpec((1, SB_WIDTH, ATT_BLOCK), lambda bi, q: (bi, 0, q)),
        out_shape=jax.ShapeDtypeStruct((b, SB_WIDTH, s), F32),
        scratch_shapes=[
            pltpu.VMEM((SB_HEADS, LANES, ATT_BLOCK), BF16),
            pltpu.VMEM((SB_HEADS, SB_HEAD_DIM, ATT_BLOCK), F32),
            pltpu.VMEM((SB_HEADS, ATT_BLOCK), F32),
            pltpu.VMEM((ATT_UNROLL, SB_HEADS, ATT_BLOCK), F32),
            pltpu.VMEM((ATT_UNROLL, SB_HEADS, ATT_BLOCK, ATT_BLOCK), F32),
            pltpu.VMEM((ATT_UNROLL, SB_HEADS, ATT_BLOCK, ATT_BLOCK), BF16),
            pltpu.VMEM((ATT_UNROLL, SB_HEADS, ATT_BLOCK, ATT_BLOCK), BF16),
        ],
        compiler_params=pltpu.CompilerParams(vmem_limit_bytes=VMEM_LIMIT),
        name="sb_attention",
    )(qT, k, vT, tri)


def _memkv_kernel(mem_ref, g_ref, w_ref, k_ref, v_ref):
    mn = _rms(mem_ref[0], g_ref[...]).astype(BF16)
    kv = _dot(mn, w_ref[...])
    k_ref[0] = kv[:, :D_MODEL].astype(BF16)
    v_ref[0] = kv[:, D_MODEL:].astype(BF16)


def _mem_kv(mem, g_mem, w_kv):
    b, m, d = mem.shape
    return pl.pallas_call(
        _memkv_kernel,
        grid=(b,),
        in_specs=[
            pl.BlockSpec((1, m, d), lambda bi: (bi, 0, 0)),
            pl.BlockSpec((1, d), lambda bi: (0, 0)),
            pl.BlockSpec((d, 2 * d), lambda bi: (0, 0)),
        ],
        out_specs=[pl.BlockSpec((1, m, d), lambda bi: (bi, 0, 0))] * 2,
        out_shape=[jax.ShapeDtypeStruct((b, m, d), BF16)] * 2,
        compiler_params=pltpu.CompilerParams(vmem_limit_bytes=VMEM_LIMIT),
        name="mem_kv",
    )(mem, g_mem.reshape(1, d), w_kv)


def _post_kernel(sbT_ref, gb_ref, cx_ref, halo_ref, x_ref, km_ref, vm_ref, cw_ref,
                 gsb_ref, gcv_ref, wout_ref, gx_ref, wq_ref, wo_ref, gffn_ref,
                 h2_ref, n3_ref):
    i = pl.program_id(1)
    tm = TOK_TILE
    sbn = _rms(sbT_ref[0].T, gsb_ref[...])

    cx = cx_ref[0]
    halo = jnp.where(i > 0, halo_ref[0], 0.0)
    row = lax.broadcasted_iota(jnp.int32, (tm, CONV_CH), 0)
    cx1 = jnp.where(row == 0, halo[7:8], pltpu.roll(cx, 1, 0))
    cx2 = jnp.where(row == 0, halo[6:7],
                    jnp.where(row == 1, halo[7:8], pltpu.roll(cx, 2, 0)))
    cw = cw_ref[...]
    conv = gb_ref[0] * (cw[0:1] * cx2 + cw[1:2] * cx1 + cw[2:3] * cx)
    cvn = _rms(conv, gcv_ref[...])

    h1 = (x_ref[0] + _dot(sbn.astype(BF16), wout_ref[0:SB_WIDTH, :])
          + _dot(cvn.astype(BF16), wout_ref[SB_WIDTH:, :]))

    n2 = _rms(h1, gx_ref[...]).astype(BF16)
    qm = _dot(n2, wq_ref[...]) * (MEM_HEAD_DIM ** -0.5)
    heads = []
    for h in range(MEM_HEADS):
        sl = slice(h * MEM_HEAD_DIM, (h + 1) * MEM_HEAD_DIM)
        sc = _dot_nt(qm[:, sl].astype(BF16), km_ref[0, :, sl])
        p = jnp.exp(sc - jnp.max(sc, axis=-1, keepdims=True))
        p = p / jnp.sum(p, axis=-1, keepdims=True)
        heads.append(_dot(p.astype(BF16), vm_ref[0, :, sl]).astype(BF16))
    h2 = h1 + _dot(jnp.concatenate(heads, axis=1), wo_ref[...])
    h2_ref[0] = h2
    n3_ref[0] = _rms(h2, gffn_ref[...]).astype(BF16)


def _post_mixer(sbT, gb, cx, x, kmem, vmem, conv_w, g_sb, g_cv, w_out, g_x, w_q, w_o, g_ffn):
    b, s, d = x.shape
    nt = s // TOK_TILE
    m = kmem.shape[1]
    hb = TOK_TILE // SUBLANES
    const = lambda shape: pl.BlockSpec(shape, lambda bi, i: (0,) * len(shape))
    return pl.pallas_call(
        _post_kernel,
        grid=(b, nt),
        in_specs=[
            pl.BlockSpec((1, SB_WIDTH, TOK_TILE), lambda bi, i: (bi, 0, i)),
            pl.BlockSpec((1, TOK_TILE, CONV_CH), lambda bi, i: (bi, i, 0)),
            pl.BlockSpec((1, TOK_TILE, CONV_CH), lambda bi, i: (bi, i, 0)),
            pl.BlockSpec((1, SUBLANES, CONV_CH), lambda bi, i: (bi, jnp.maximum(i * hb - 1, 0), 0)),
            pl.BlockSpec((1, TOK_TILE, d), lambda bi, i: (bi, i, 0)),
            pl.BlockSpec((1, m, d), lambda bi, i: (bi, 0, 0)),
            pl.BlockSpec((1, m, d), lambda bi, i: (bi, 0, 0)),
            const((CONV_K, CONV_CH)),
            const((1, SB_WIDTH)),
            const((1, CONV_CH)),
            const((d, d)),
            const((1, d)),
            const((d, d)),
            const((d, d)),
            const((1, d)),
        ],
        out_specs=[
            pl.BlockSpec((1, TOK_TILE, d), lambda bi, i: (bi, i, 0)),
            pl.BlockSpec((1, TOK_TILE, d), lambda bi, i: (bi, i, 0)),
        ],
        out_shape=[
            jax.ShapeDtypeStruct((b, s, d), F32),
            jax.ShapeDtypeStruct((b, s, d), BF16),
        ],
        compiler_params=pltpu.CompilerParams(vmem_limit_bytes=VMEM_LIMIT),
        name="post_mixer",
    )(sbT, gb, cx, cx, x, kmem, vmem, conv_w, g_sb.reshape(1, -1), g_cv.reshape(1, -1), w_out,
      g_x.reshape(1, d), w_q, w_o, g_ffn.reshape(1, d))


def _top16(s):
    nk, nt = s.shape
    key_id = lax.broadcasted_iota(jnp.int32, (nk, nt), 0).astype(F32)
    slot = lax.broadcasted_iota(jnp.int32, (PEER_TOPK, nt), 0)
    cur = s
    rank = jnp.full((nk, nt), float(PEER_TOPK), F32)
    top = jnp.zeros((PEER_TOPK, nt), F32)
    for r in range(PEER_TOPK):
        m = jnp.max(cur, axis=0, keepdims=True)
        first = jnp.min(jnp.where(cur == m, key_id, float(nk)), axis=0, keepdims=True)
        sel = key_id == first
        rank = jnp.where(sel, float(r), rank)
        cur = jnp.where(sel, -jnp.inf, cur)
        top = jnp.where(slot == r, m, top)
    return rank, top


def _top16_distinct(s, want_rank):
    nk, nt = s.shape
    slot = lax.broadcasted_iota(jnp.int32, (PEER_TOPK, nt), 0)
    cur = s
    rank = jnp.full((nk, nt), float(PEER_TOPK), F32)
    top = jnp.zeros((PEER_TOPK, nt), F32)
    for r in range(PEER_TOPK):
        m = jnp.max(cur, axis=0, keepdims=True)
        hit = cur == m
        if want_rank:
            rank = jnp.where(hit, float(r), rank)
        cur = jnp.where(hit, -jnp.inf, cur)
        top = jnp.where(slot == r, m, top)
    taken = jnp.sum(jnp.where(cur == -jnp.inf, 1.0, 0.0), axis=0, keepdims=True)
    return rank, top, jnp.abs(taken - float(PEER_TOPK))


def _staircase(top1, top2):
    nt = top1.shape[1]
    row = lax.broadcasted_iota(jnp.int32, (PEER_TOPK, nt), 0).astype(F32)
    cnt = jnp.zeros((PEER_TOPK, nt), F32)
    front = top1 + top2[0:1]
    for _ in range(PEER_TOPK):
        m = jnp.max(front, axis=0, keepdims=True)
        a = jnp.min(jnp.where(front == m, row, float(PEER_TOPK)), axis=0, keepdims=True)
        sel = row == a
        cnt = jnp.where(sel, cnt + 1.0, cnt)
        nxt = jnp.sum(jnp.where(sel, cnt, 0.0), axis=0, keepdims=True)
        val = jnp.sum(jnp.where(row == nxt, top2, 0.0), axis=0, keepdims=True)
        val = jnp.where(nxt >= float(PEER_TOPK), -jnp.inf, val)
        front = jnp.where(sel, top1 + val, front)
    return cnt


def _gelu(x):
    return 0.5 * x * (1.0 + lax.erf(x * (2.0 ** -0.5)))


def _peer_kernel(n3_ref, h2_ref, wqT_ref, sk_ref, u0_ref, u_ref, vT_ref, gf_ref, out_ref,
                 acc_ref, act_ref, coef_ref, qT_ref, s_ref, rb_ref, eb_ref, ci_ref, ea_ref):
    c = pl.program_id(1)
    nchunks = pl.num_programs(1) - 1

    @pl.when(c == 0)
    def _prepare():
        acc_ref[...] = jnp.zeros_like(acc_ref)
        qT_ref[...] = _dot_nt(wqT_ref[...], n3_ref[...]).astype(BF16)
        for hc in range(2 * PEER_HEADS):
            s = _dot(sk_ref[hc], qT_ref[hc * PEER_HALF:(hc + 1) * PEER_HALF, :])
            for lc in range(PEER_LC):
                s_ref[hc, lc] = s[:, lc * LANES:(lc + 1) * LANES]

        def select(h, lc, exact):
            s1 = s_ref[2 * h, lc]
            s2 = s_ref[2 * h + 1, lc]
            if exact:
                rank1, top1 = _top16(s1)
                rank2, top2 = _top16(s2)
                tie = None
            else:
                _, top1, tie1 = _top16_distinct(s1, False)
                rank2, top2, tie2 = _top16_distinct(s2, True)
                tie = jnp.maximum(tie1, tie2)
            cnt = _staircase(top1, top2)
            m1 = top1[0:1]
            m2 = top2[0:1]
            e1 = jnp.exp(top1 - m1)
            e2 = jnp.exp(top2 - m2)
            pref = jnp.zeros_like(cnt)
            for b in range(PEER_TOPK):
                pref = pref + jnp.where(cnt > float(b), e2[b:b + 1], 0.0)
            z = jnp.sum(e1 * pref, axis=0, keepdims=True)
            ci = jnp.zeros_like(s1)
            for a in range(PEER_TOPK):
                in_row = (rank1 == float(a)) if exact else (s1 == top1[a:a + 1])
                ci = jnp.where(in_row, cnt[a:a + 1], ci)
            rb_ref[h, lc] = rank2.astype(BF16)
            eb_ref[h, lc] = jnp.exp(s2 - m2).astype(BF16)
            ci_ref[h, lc] = ci
            ea_ref[h, lc] = jnp.exp(s1 - m1) / z
            return tie

        def per_head(h, carry):
            tie = select(h, 0, False)
            for lc in range(1, PEER_LC):
                tie = jnp.maximum(tie, select(h, lc, False))

            @pl.when(jnp.max(tie) > 0.0)
            def _with_ties():
                def slow(lc, inner):
                    select(h, lc, True)
                    return inner

                lax.fori_loop(0, PEER_LC, slow, 0)

            return carry

        lax.fori_loop(0, PEER_HEADS, per_head, 0)

    cur = c % 2
    nxt = 1 - cur

    def activations(u_rows_ref):
        return _gelu(_dot_nt(u_rows_ref[...], n3_ref[...])).astype(BF16)

    @pl.when(c == 0)
    def _first():
        coef_ref[1] = jnp.zeros((PEER_ECHUNK, PEER_TOK), BF16)
        act_ref[0] = activations(u0_ref)

    @pl.when(c < nchunks)
    def _chunk():
        i0 = pl.multiple_of(c * PEER_ROWS, PEER_ROWS)
        for lc in range(PEER_LC):
            ls = slice(lc * LANES, (lc + 1) * LANES)
            ur = slice(lc * PEER_ECHUNK // PEER_LC, (lc + 1) * PEER_ECHUNK // PEER_LC)
            dr = slice(lc * D_MODEL // PEER_LC, (lc + 1) * D_MODEL // PEER_LC)
            act_ref[nxt, ur, :] = activations(u_ref.at[ur, :])
            acc_ref[dr, :] += _dot(vT_ref[dr, :], coef_ref[nxt])
            for k0 in range(0, PEER_KEYS, PEER_KSLAB):
                ks = slice(k0, k0 + PEER_KSLAB)
                gates = [jnp.zeros((PEER_KSLAB, LANES), BF16) for _ in range(PEER_ROWS)]
                for h in range(PEER_HEADS):
                    rb = rb_ref[h, lc, ks, :]
                    eb = eb_ref[h, lc, ks, :]
                    ci8 = ci_ref[h, lc, pl.ds(i0, PEER_ROWS), :]
                    ea8 = ea_ref[h, lc, pl.ds(i0, PEER_ROWS), :]
                    for il in range(PEER_ROWS):
                        keep = rb < ci8[il:il + 1, :].astype(BF16)
                        w = eb * ea8[il:il + 1, :].astype(BF16)
                        gates[il] = gates[il] + jnp.where(keep, w, jnp.zeros_like(w))
                for il in range(PEER_ROWS):
                    rs = slice(il * PEER_KEYS + k0, il * PEER_KEYS + k0 + PEER_KSLAB)
                    coef_ref[cur, rs, ls] = gates[il] * act_ref[cur, rs, ls]

    @pl.when(c == nchunks)
    def _finish():
        acc = acc_ref[...] + _dot(vT_ref[...], coef_ref[nxt])
        out_ref[...] = _rms(h2_ref[...] + acc.T, gf_ref[...])


def _peer(n3, h2, wqT, sk, u, vT, g_final):
    t, d = n3.shape
    nt = t // PEER_TOK
    nc = PEER_EXPERTS // PEER_ECHUNK
    tile4 = (PEER_HEADS, PEER_LC, PEER_KEYS, LANES)
    return pl.pallas_call(
        _peer_kernel,
        grid=(nt, nc + 1),
        in_specs=[
            pl.BlockSpec((PEER_TOK, d), lambda i, c: (i, 0)),
            pl.BlockSpec((PEER_TOK, d), lambda i, c: (i, 0)),
            pl.BlockSpec((PEER_HEADS * PEER_QDIM, d), lambda i, c: (0, 0)),
            pl.BlockSpec((2 * PEER_HEADS, PEER_KEYS, PEER_HALF), lambda i, c: (0, 0, 0)),
            pl.BlockSpec((PEER_ECHUNK, d), lambda i, c: (0, 0)),
            pl.BlockSpec((PEER_ECHUNK, d), lambda i, c: (jnp.minimum(c + 1, nc - 1), 0)),
            pl.BlockSpec((d, PEER_ECHUNK), lambda i, c: (0, jnp.maximum(c - 1, 0))),
            pl.BlockSpec((1, d), lambda i, c: (0, 0)),
        ],
        out_specs=pl.BlockSpec((PEER_TOK, d), lambda i, c: (i, 0)),
        out_shape=jax.ShapeDtypeStruct((t, d), F32),
        scratch_shapes=[
            pltpu.VMEM((d, PEER_TOK), F32),
            pltpu.VMEM((2, PEER_ECHUNK, PEER_TOK), BF16),
            pltpu.VMEM((2, PEER_ECHUNK, PEER_TOK), BF16),
            pltpu.VMEM((PEER_HEADS * PEER_QDIM, PEER_TOK), BF16),
            pltpu.VMEM((2 * PEER_HEADS, PEER_LC, PEER_KEYS, LANES), F32),
            pltpu.VMEM(tile4, BF16),
            pltpu.VMEM(tile4, BF16),
            pltpu.VMEM(tile4, F32),
            pltpu.VMEM(tile4, F32),
        ],
        compiler_params=pltpu.CompilerParams(vmem_limit_bytes=VMEM_LIMIT),
        name="peer",
    )(n3, h2, wqT, sk, u, u, vT, g_final.reshape(1, d))


def kernel(x, mem, g_mix, w_in, conv_w, g_sb_out, g_conv_out, w_out, g_xattn, g_mem, w_q_mem, w_kv_mem, w_o_mem, g_ffn, w_query, sub_keys, expert_u, expert_v, g_final):
    b, s, d = x.shape
    assert g_mix.shape[0] == 1, "the final rmsnorm is fused into the single layer's PEER kernel"
    l = 0
    qT, k, vT, gb, cx = _in_proj(x, g_mix[l], w_in[l].astype(BF16))
    sbT = _attention(qT, k, vT)
    kmem, vmem = _mem_kv(mem, g_mem[l], w_kv_mem[l].astype(BF16))
    h2, n3 = _post_mixer(sbT, gb, cx, x, kmem, vmem, conv_w[l], g_sb_out[l], g_conv_out[l],
                         w_out[l].astype(BF16), g_xattn[l], w_q_mem[l].astype(BF16),
                         w_o_mem[l].astype(BF16), g_ffn[l])
    out = _peer(n3.reshape(b * s, d), h2.reshape(b * s, d),
                w_query[l].T.astype(BF16),
                sub_keys[l].reshape(2 * PEER_HEADS, PEER_KEYS, PEER_HALF).astype(BF16),
                expert_u[l].astype(BF16), expert_v[l].T.astype(BF16), g_final)
    return out.reshape(b, s, d)
```
